```python
import math
import jax, jax.numpy as jnp
from jax import lax
import numpy as np

D_MODEL = 1024
BATCH = 16
SEQ = 2048
DEPTH = 2

GRID_W = 64
CTX_LEN = 256
ROPE_THETA = 10000.0
NORM_EPS = 1e-6
SUBLN_EPS = 1e-5
LN_EPS = 1e-5
Q_BLOCK = 128

A_GROUPS = 8
A_WIDTH = D_MODEL // 2
A_GROUP_DIM = A_WIDTH // A_GROUPS
A_CHUNK = 128

B_HEADS = 4
B_HEAD_DIM = 64
B_V_DIM = 2 * B_HEAD_DIM
B_WIDTH = B_HEADS * B_V_DIM
B_QK = B_HEADS * 2 * B_HEAD_DIM

EVEN_IN = 3 * A_WIDTH + 2 * B_QK + 2 * B_WIDTH
EVEN_MIX = A_WIDTH + B_WIDTH
EVEN_SPLITS = (A_WIDTH, 2 * A_WIDTH, 3 * A_WIDTH, 3 * A_WIDTH + B_QK,
               3 * A_WIDTH + 2 * B_QK, 3 * A_WIDTH + 2 * B_QK + B_WIDTH)

C_HEADS = 8
C_NOPE = 128
C_ROPE = 64
C_V = 128
C_Q_RANK = 256
C_KV_RANK = 128
C_WIDTH = C_HEADS * C_V
ODD_IN = C_Q_RANK + C_KV_RANK + C_ROPE + C_WIDTH
ODD_SPLITS = (C_Q_RANK, C_Q_RANK + C_KV_RANK, C_Q_RANK + C_KV_RANK + C_ROPE)

N_EVEN = (DEPTH + 1) // 2
N_ODD = DEPTH // 2

kernel_name = 'hybrid_gmlp_diffattn_mla_prefix_dit'

F32 = jnp.float32


def rms_norm(x, w, eps=NORM_EPS):
    xf = x.astype(F32)
    y = xf * lax.rsqrt(jnp.mean(xf * xf, axis=-1, keepdims=True) + eps)
    return (y * w.astype(F32)).astype(x.dtype)


def layer_norm(x, w, b, eps=LN_EPS):
    xf = x.astype(F32)
    mu = jnp.mean(xf, axis=-1, keepdims=True)
    xc = xf - mu
    y = xc * lax.rsqrt(jnp.mean(xc * xc, axis=-1, keepdims=True) + eps)
    return (y * w.astype(F32) + b.astype(F32)).astype(x.dtype)


def adaln_terms(cond, w_ada, b_ada):
    m = jax.nn.silu(cond) @ w_ada + b_ada
    return jnp.split(m, 3, axis=-1)


def axial_rope_tables(seq, dim):
    rows = seq // GRID_W
    row = jnp.repeat(jnp.arange(rows), GRID_W).astype(F32)
    col = jnp.tile(jnp.arange(GRID_W), rows).astype(F32)
    half = dim // 2
    inv = ROPE_THETA ** (-jnp.arange(0, half, 2, dtype=F32) / half)
    ang_r = row[:, None] * inv[None, :]
    ang_c = col[:, None] * inv[None, :]
    ang = jnp.concatenate([ang_r, ang_r, ang_c, ang_c], axis=-1)
    return jnp.cos(ang), jnp.sin(ang)


def apply_rope(x, cos, sin):
    extra = x.ndim - 3
    shp = (1, cos.shape[0]) + (1,) * extra + (cos.shape[1],)
    c = cos.reshape(shp)
    s = sin.reshape(shp)
    seg = x.reshape(x.shape[:-1] + (2, 2, x.shape[-1] // 4))
    rot = jnp.concatenate([-seg[..., 1:, :], seg[..., :1, :]], axis=-2).reshape(x.shape)
    return (x.astype(F32) * c + rot.astype(F32) * s).astype(x.dtype)


def sweep_query_blocks(fn, q):
    b, s = q.shape[0], q.shape[1]
    nb = s // Q_BLOCK
    qb = jnp.moveaxis(q.reshape((b, nb, Q_BLOCK) + q.shape[2:]), 1, 0)
    ob = lax.map(fn, qb)
    return jnp.moveaxis(ob, 0, 1).reshape((b, s) + ob.shape[3:])


def spatial_gating(u, v, w_s, b_s, ln_w, ln_b):
    bsz, length, _ = v.shape
    vn = layer_norm(v, ln_w, ln_b)
    vc = vn.reshape(bsz, length // A_CHUNK, A_CHUNK, A_GROUPS, A_GROUP_DIM)
    mixed = jnp.einsum('gpq,bnqgd->bnpgd', w_s, vc) + b_s.T[None, None, :, :, None]
    return u * mixed.reshape(bsz, length, A_WIDTH)


def diff_attn_core(q, k, v, lam):
    s = jnp.einsum('bqhmd,bkhmd->bhmqk', q, k).astype(F32) * (B_HEAD_DIM ** -0.5)
    p = jax.nn.softmax(s, axis=-1)
    a = p[:, :, 0] - lam * p[:, :, 1]
    return jnp.einsum('bhqk,bkhd->bqhd', a.astype(v.dtype), v)


def mla_core(q, k, v):
    s = jnp.einsum('bqhd,bkhd->bhqk', q, k).astype(F32) * ((C_NOPE + C_ROPE) ** -0.5)
    p = jax.nn.softmax(s, axis=-1)
    return jnp.einsum('bhqk,bkhd->bqhd', p.astype(v.dtype), v)


def even_mixer(xl, xc, need_ctx, li, w_in, w_s, b_s, ln_w, ln_b, lq1, lk1, lq2, lk2, subln_w, w_out):
    bsz, seq, _ = xl.shape
    clen = xc.shape[1]
    cos, sin = axial_rope_tables(seq, B_HEAD_DIM)
    lam_init = 0.8 - 0.6 * math.exp(-0.3 * li)
    lam = (jnp.exp(jnp.sum(lq1.astype(F32) * lk1.astype(F32)))
           - jnp.exp(jnp.sum(lq2.astype(F32) * lk2.astype(F32))) + lam_init)

    au_l, av_l, az_l, bq_l, bk_l, bv_l, bz_l = jnp.split(xl @ w_in, EVEN_SPLITS, axis=-1)
    au_c, av_c, az_c, bq_c, bk_c, bv_c, bz_c = jnp.split(xc @ w_in, EVEN_SPLITS, axis=-1)

    def a_branch(u, v, z):
        return spatial_gating(jax.nn.gelu(u, approximate=False), jax.nn.gelu(v, approximate=False),
                              w_s, b_s, ln_w, ln_b) * jax.nn.silu(z)

    def b_post(o, z, length):
        o = rms_norm(o, subln_w, SUBLN_EPS) * (1.0 - lam_init)
        return o.reshape(bsz, length, B_WIDTH) * jax.nn.silu(z)

    qk_shape_l = (bsz, seq, B_HEADS, 2, B_HEAD_DIM)
    qk_shape_c = (bsz, clen, B_HEADS, 2, B_HEAD_DIM)
    q_l = apply_rope(bq_l.reshape(qk_shape_l), cos, sin)
    k_l = apply_rope(bk_l.reshape(qk_shape_l), cos, sin)
    v_l = bv_l.reshape(bsz, seq, B_HEADS, B_V_DIM)
    k_c = bk_c.reshape(qk_shape_c)
    v_c = bv_c.reshape(bsz, clen, B_HEADS, B_V_DIM)
    k_all = jnp.concatenate([k_c, k_l], axis=1)
    v_all = jnp.concatenate([v_c, v_l], axis=1)

    ob_l = sweep_query_blocks(lambda qb: diff_attn_core(qb, k_all, v_all, lam), q_l)
    out_l = jnp.concatenate([a_branch(au_l, av_l, az_l), b_post(ob_l, bz_l, seq)], axis=-1) @ w_out

    out_c = None
    if need_ctx:
        q_c = bq_c.reshape(qk_shape_c)
        ob_c = diff_attn_core(q_c, k_c, v_c, lam)
        out_c = jnp.concatenate([a_branch(au_c, av_c, az_c), b_post(ob_c, bz_c, clen)], axis=-1) @ w_out
    return out_l, out_c


def odd_mixer(xl, xc, need_ctx, w_in, q_norm_w, wq_b, kv_norm_w, wkv_b, w_out):
    bsz, seq, _ = xl.shape
    clen = xc.shape[1]
    cos, sin = axial_rope_tables(seq, C_ROPE)

    cq_l, ckv_l, kr_l, z_l = jnp.split(xl @ w_in, ODD_SPLITS, axis=-1)
    cq_c, ckv_c, kr_c, z_c = jnp.split(xc @ w_in, ODD_SPLITS, axis=-1)

    def queries(cq, length):
        q = (rms_norm(cq, q_norm_w) @ wq_b).reshape(bsz, length, C_HEADS, C_NOPE + C_ROPE)
        return q[..., :C_NOPE], q[..., C_NOPE:]

    def keys_values(ckv, k_rope, length):
        kv = (rms_norm(ckv, kv_norm_w) @ wkv_b).reshape(bsz, length, C_HEADS, C_NOPE + C_V)
        k_rope_h = jnp.broadcast_to(k_rope[:, :, None, :], (bsz, length, C_HEADS, C_ROPE))
        return jnp.concatenate([kv[..., :C_NOPE], k_rope_h], axis=-1), kv[..., C_NOPE:]

    qn_l, qr_l = queries(cq_l, seq)
    q_l = jnp.concatenate([qn_l, apply_rope(qr_l, cos, sin)], axis=-1)
    kr_l_rot = apply_rope(kr_l[:, :, None, :], cos, sin)[:, :, 0, :]
    k_l, v_l = keys_values(ckv_l, kr_l_rot, seq)
    k_c, v_c = keys_values(ckv_c, kr_c, clen)
    k_all = jnp.concatenate([k_c, k_l], axis=1)
    v_all = jnp.concatenate([v_c, v_l], axis=1)

    o_l = sweep_query_blocks(lambda qb: mla_core(qb, k_all, v_all), q_l)
    out_l = (o_l.reshape(bsz, seq, C_WIDTH) * jax.nn.silu(z_l)) @ w_out

    out_c = None
    if need_ctx:
        qn_c, qr_c = queries(cq_c, clen)
        q_c = jnp.concatenate([qn_c, qr_c], axis=-1)
        o_c = mla_core(q_c, k_c, v_c)
        out_c = (o_c.reshape(bsz, clen, C_WIDTH) * jax.nn.silu(z_c)) @ w_out
    return out_l, out_c


def setup_inputs(seed: int = 0) -> dict:
    key = jax.random.key(seed)
    ks = jax.random.split(key, 32)
    nrm = jax.random.normal
    D = D_MODEL
    return {
        'x': nrm(ks[0], (BATCH, SEQ, D), F32),
        'c': nrm(ks[1], (BATCH, D), F32),
        'ctx': nrm(ks[2], (BATCH, CTX_LEN, D), F32),
        'c_ctx': nrm(ks[3], (D,), F32),
        'norm_w': 1.0 + 0.02 * nrm(ks[4], (DEPTH, D), F32),
        'ada_w': 0.5 * D ** -0.5 * nrm(ks[5], (DEPTH, D, 3 * D), F32),
        'ada_b': 0.01 * nrm(ks[6], (DEPTH, 3 * D), F32),
        'even_w_in': D ** -0.5 * nrm(ks[7], (N_EVEN, D, EVEN_IN), F32),
        'a_ws': A_CHUNK ** -0.5 * nrm(ks[8], (N_EVEN, A_GROUPS, A_CHUNK, A_CHUNK), F32),
        'a_bs': 1.0 + 0.02 * nrm(ks[9], (N_EVEN, A_GROUPS, A_CHUNK), F32),
        'a_ln_w': 1.0 + 0.02 * nrm(ks[10], (N_EVEN, A_WIDTH), F32),
        'a_ln_b': 0.01 * nrm(ks[11], (N_EVEN, A_WIDTH), F32),
        'b_lq1': 0.1 * nrm(ks[12], (N_EVEN, B_HEAD_DIM), F32),
        'b_lk1': 0.1 * nrm(ks[13], (N_EVEN, B_HEAD_DIM), F32),
        'b_lq2': 0.1 * nrm(ks[14], (N_EVEN, B_HEAD_DIM), F32),
        'b_lk2': 0.1 * nrm(ks[15], (N_EVEN, B_HEAD_DIM), F32),
        'b_subln_w': 1.0 + 0.02 * nrm(ks[16], (N_EVEN, B_V_DIM), F32),
        'even_w_out': EVEN_MIX ** -0.5 * nrm(ks[17], (N_EVEN, EVEN_MIX, D), F32),
        'odd_w_in': D ** -0.5 * nrm(ks[18], (N_ODD, D, ODD_IN), F32),
        'c_q_norm_w': 1.0 + 0.02 * nrm(ks[19], (N_ODD, C_Q_RANK), F32),
        'c_wq_b': C_Q_RANK ** -0.5 * nrm(ks[20], (N_ODD, C_Q_RANK, C_HEADS * (C_NOPE + C_ROPE)), F32),
        'c_kv_norm_w': 1.0 + 0.02 * nrm(ks[21], (N_ODD, C_KV_RANK), F32),
        'c_wkv_b': C_KV_RANK ** -0.5 * nrm(ks[22], (N_ODD, C_KV_RANK, C_HEADS * (C_NOPE + C_V)), F32),
        'odd_w_out': C_WIDTH ** -0.5 * nrm(ks[23], (N_ODD, C_WIDTH, D), F32),
        'final_w': 1.0 + 0.02 * nrm(ks[24], (D,), F32),
    }


def reference(x, c, ctx, c_ctx, norm_w, ada_w, ada_b, even_w_in, a_ws, a_bs, a_ln_w, a_ln_b,
              b_lq1, b_lk1, b_lq2, b_lk2, b_subln_w, even_w_out, odd_w_in, c_q_norm_w, c_wq_b,
              c_kv_norm_w, c_wkv_b, odd_w_out, final_w):
    h_lat, h_ctx = x, ctx
    for li in range(DEPTH):
        need_ctx = li < DEPTH - 1
        shift_l, scale_l, gate_l = adaln_terms(c, ada_w[li], ada_b[li])
        shift_c, scale_c, gate_c = adaln_terms(c_ctx, ada_w[li], ada_b[li])
        xl = rms_norm(h_lat, norm_w[li]) * (1.0 + scale_l[:, None, :]) + shift_l[:, None, :]
        xc = rms_norm(h_ctx, norm_w[li]) * (1.0 + scale_c) + shift_c
        if li % 2 == 0:
            e = li // 2
            out_l, out_c = even_mixer(xl, xc, need_ctx, li, even_w_in[e], a_ws[e], a_bs[e],
                                      a_ln_w[e], a_ln_b[e], b_lq1[e], b_lk1[e], b_lq2[e], b_lk2[e],
                                      b_subln_w[e], even_w_out[e])
        else:
            o = li // 2
            out_l, out_c = odd_mixer(xl, xc, need_ctx, odd_w_in[o], c_q_norm_w[o], c_wq_b[o],
                                     c_kv_norm_w[o], c_wkv_b[o], odd_w_out[o])
        h_lat = h_lat + gate_l[:, None, :] * out_l
        if need_ctx:
            h_ctx = h_ctx + gate_c * out_c
    return rms_norm(h_lat, final_w)
```

```python
import functools
import math

import jax
import jax.numpy as jnp
from jax import lax
from jax.experimental import pallas as pl
from jax.experimental.pallas import tpu as pltpu

F32 = jnp.float32
BF16 = jnp.bfloat16

GRID_W = 64
ROPE_THETA = 10000.0
NORM_EPS = 1e-6
SUBLN_EPS = 1e-5
LN_EPS = 1e-5
A_GROUPS = 8
A_CHUNK = 128
B_HEADS = 4
B_HEAD_DIM = 64
C_HEADS = 8
C_NOPE = 128
C_ROPE = 64
C_V = 128
C_Q_RANK = 256
C_KV_RANK = 128

LANES = 128
V7X_VMEM_LIMIT_CAP = 56 << 20
MOD_ROWS = 24

ROPE_SEG = 32
ROPE_HALF = 16


def _vmem_limit(block_bytes, temp_bytes):
    return int(min(2 * block_bytes + temp_bytes, V7X_VMEM_LIMIT_CAP))


def _nbytes(shape, dtype):
    return math.prod(shape) * jnp.dtype(dtype).itemsize


def _silu(x):
    return x * (1.0 / (1.0 + jnp.exp(-x)))


def _gelu(x):
    return 0.5 * x * (1.0 + lax.erf(x * (2.0 ** -0.5)))


def _rms(x, w, eps):
    return x * lax.rsqrt(jnp.mean(x * x, axis=-1, keepdims=True) + eps) * w


def _dot(a, b):
    return jnp.dot(a, b, preferred_element_type=F32)


def _dot_nt(a, b):
    return lax.dot_general(a, b, (((1,), (1,)), ((), ())), preferred_element_type=F32)


def _rope128(x, cos, sin_lo, sin_hi):
    return (x * cos + pltpu.roll(x, LANES - ROPE_HALF, 1) * sin_lo
            + pltpu.roll(x, ROPE_HALF, 1) * sin_hi)


def _rope_tables(seq, dim):
    rows = seq // GRID_W
    row = jnp.repeat(jnp.arange(rows), GRID_W).astype(F32)
    col = jnp.tile(jnp.arange(GRID_W), rows).astype(F32)
    half = dim // 2
    inv = ROPE_THETA ** (-jnp.arange(0, half, 2, dtype=F32) / half)
    ang_r = row[:, None] * inv[None, :]
    ang_c = col[:, None] * inv[None, :]
    ang = jnp.concatenate([ang_r, ang_r, ang_c, ang_c], axis=-1)
    ang = jnp.tile(ang, (1, LANES // dim))
    first_half = (jnp.arange(LANES) % ROPE_SEG) < ROPE_HALF
    sin = jnp.sin(ang)
    return (jnp.cos(ang), jnp.where(first_half, -sin, 0.0), jnp.where(first_half, 0.0, sin))


def _flash(q, kv_blocks):
    m = l = acc = None
    for kb, vb in kv_blocks:
        s = _dot_nt(q, kb)
        mb = jnp.max(s, axis=-1, keepdims=True)
        if m is None:
            m_new = mb
            p = jnp.exp(s - m_new)
            l = jnp.sum(p, axis=-1, keepdims=True)
            acc = _dot(p.astype(BF16), vb)
        else:
            m_new = jnp.maximum(m, mb)
            alpha = jnp.exp(m - m_new)
            p = jnp.exp(s - m_new)
            l = alpha * l + jnp.sum(p, axis=-1, keepdims=True)
            acc = alpha * acc + _dot(p.astype(BF16), vb)
        m = m_new
    return acc, l


def _key_blocks(n_ctx, n_lat, block):
    out = [("ctx", 0, n_ctx)]
    out += [("lat", s, block) for s in range(0, n_lat, block)]
    return out


def _mod_kernel(c_ref, w_ref, b_ref, o_ref):
    a = _silu(c_ref[...])
    w = w_ref[0]
    a_hi = a.astype(BF16)
    a_lo = (a - a_hi.astype(F32)).astype(BF16)
    w_hi = w.astype(BF16)
    w_lo = (w - w_hi.astype(F32)).astype(BF16)
    o_ref[0] = _dot(a_hi, w_hi) + (_dot(a_lo, w_hi) + _dot(a_hi, w_lo)) + b_ref[0]


def _modulation(cond, ada_w, ada_b):
    depth, d, n = ada_w.shape
    tn = 768
    return pl.pallas_call(
        _mod_kernel,
        grid=(depth, n // tn),
        in_specs=[
            pl.BlockSpec((MOD_ROWS, d), lambda l, j: (0, 0)),
            pl.BlockSpec((1, d, tn), lambda l, j: (l, 0, j)),
            pl.BlockSpec((1, 1, tn), lambda l, j: (l, 0, j)),
        ],
        out_specs=pl.BlockSpec((1, MOD_ROWS, tn), lambda l, j: (l, 0, j)),
        out_shape=jax.ShapeDtypeStruct((depth, MOD_ROWS, n), F32),
        compiler_params=pltpu.CompilerParams(
            dimension_semantics=("arbitrary", "arbitrary"),
            vmem_limit_bytes=_vmem_limit(_nbytes((d, tn), F32), 16 << 20)),
        name="modulation",
    )(cond, ada_w, ada_b.reshape(depth, 1, n))


def _modulated_input(h, mod_ref, row, normw_ref, d):
    shift = mod_ref[pl.ds(row, 1), 0:d]
    scale = mod_ref[pl.ds(row, 1), d:2 * d]
    xn = _rms(h, normw_ref[...], NORM_EPS)
    return (xn * (1.0 + scale) + shift).astype(BF16)


def _even_in_kernel(*refs, use_rope, ctx_row, d, aw):
    if use_rope:
        (h_ref, mod_ref, normw_ref, win_ref, ws_ref, bias_ref, lnw_ref, lnb_ref,
         cos_ref, slo_ref, shi_ref, a_ref, q_ref, k_ref, v_ref, z_ref) = refs
    else:
        (h_ref, mod_ref, normw_ref, win_ref, ws_ref, bias_ref, lnw_ref, lnb_ref,
         a_ref, q_ref, k_ref, v_ref, z_ref) = refs
    row = pl.program_id(0) if ctx_row is None else ctx_row
    xl = _modulated_input(h_ref[0], mod_ref, row, normw_ref, d)
    tm = xl.shape[0]

    au = _dot(xl, win_ref[:, 0:aw])
    av = _dot(xl, win_ref[:, aw:2 * aw])
    az = _dot(xl, win_ref[:, 2 * aw:3 * aw])
    gv = _gelu(av)
    mu = jnp.mean(gv, axis=-1, keepdims=True)
    vc = gv - mu
    vn = (vc * lax.rsqrt(jnp.mean(vc * vc, axis=-1, keepdims=True) + LN_EPS) * lnw_ref[...]
          + lnb_ref[...]).astype(BF16)
    lane = lax.broadcasted_iota(jnp.int32, (A_CHUNK, LANES), 1)
    first_group = lane < (LANES // 2)
    zero = jnp.zeros((A_CHUNK, LANES), BF16)
    chunks = []
    for c in range(tm // A_CHUNK):
        cols = []
        for p in range(aw // LANES):
            vp = vn[c * A_CHUNK:(c + 1) * A_CHUNK, p * LANES:(p + 1) * LANES]
            rhs = jnp.concatenate([jnp.where(first_group, vp, zero), jnp.where(first_group, zero, vp)], axis=0)
            cols.append(_dot(ws_ref[p], rhs))
        chunks.append(jnp.concatenate(cols, axis=1) + bias_ref[...])
    mixed = jnp.concatenate(chunks, axis=0)
    a_ref[0] = (_gelu(au) * mixed * _silu(az)).astype(BF16)

    bq = _dot(xl, win_ref[:, 3 * aw:4 * aw]) * (B_HEAD_DIM ** -0.5)
    bk = _dot(xl, win_ref[:, 4 * aw:5 * aw])
    if use_rope:
        cos, slo, shi = cos_ref[...], slo_ref[...], shi_ref[...]
        bq = jnp.concatenate([_rope128(bq[:, i * LANES:(i + 1) * LANES], cos, slo, shi)
                              for i in range(aw // LANES)], axis=1)
        bk = jnp.concatenate([_rope128(bk[:, i * LANES:(i + 1) * LANES], cos, slo, shi)
                              for i in range(aw // LANES)], axis=1)
    q_ref[0] = bq.astype(BF16)
    k_ref[0] = bk.astype(BF16)
    v_ref[0] = _dot(xl, win_ref[:, 5 * aw:6 * aw]).astype(BF16)
    z_ref[0] = _silu(_dot(xl, win_ref[:, 6 * aw:7 * aw])).astype(BF16)


def _even_in(h, mod, norm_w, w_in, ws_pairs, bias, ln_w, ln_b, rope, *, tm, ctx_row):
    bsz, rows, d = h.shape
    aw = ln_w.shape[-1]
    n_in = w_in.shape[1]
    use_rope = rope is not None
    const = lambda *_: (0, 0)
    in_specs = [
        pl.BlockSpec((1, tm, d), lambda b, j: (b, j, 0)),
        pl.BlockSpec(mod.shape, const),
        pl.BlockSpec((1, d), const),
        pl.BlockSpec((d, n_in), const),
        pl.BlockSpec(ws_pairs.shape, lambda *_: (0, 0, 0)),
        pl.BlockSpec(bias.shape, const),
        pl.BlockSpec((1, aw), const),
        pl.BlockSpec((1, aw), const),
    ]
    args = [h, mod, norm_w, w_in, ws_pairs, bias, ln_w, ln_b]
    if use_rope:
        in_specs += [pl.BlockSpec((tm, LANES), lambda b, j: (j, 0))] * 3
        args += list(rope)
    out_spec = pl.BlockSpec((1, tm, aw), lambda b, j: (b, j, 0))
    out_sds = jax.ShapeDtypeStruct((bsz, rows, aw), BF16)
    blocks = (_nbytes((tm, d), F32) + _nbytes((d, n_in), BF16) + 5 * _nbytes((tm, aw), BF16)
              + _nbytes(mod.shape, F32) + 3 * _nbytes((tm, LANES), F32))
    return pl.pallas_call(
        functools.partial(_even_in_kernel, use_rope=use_rope, ctx_row=ctx_row, d=d, aw=aw),
        grid=(bsz, rows // tm),
        in_specs=in_specs,
        out_specs=[out_spec] * 5,
        out_shape=[out_sds] * 5,
        compiler_params=pltpu.CompilerParams(
            dimension_semantics=("arbitrary", "arbitrary"),
            vmem_limit_bytes=_vmem_limit(blocks, 10 * _nbytes((tm, n_in), F32))),
        name="even_in_lat" if use_rope else "even_in_ctx",
    )(*args)


def _even_attn_kernel(*refs, with_lat_keys, ctx_row, d, lam_init, kblock):
    if with_lat_keys:
        (h_ref, mod_ref, a_ref, q_ref, z_ref, kc_ref, vc_ref, kl_ref, vl_ref,
         lq1_ref, lk1_ref, lq2_ref, lk2_ref, subw_ref, wout_ref, o_ref) = refs
    else:
        (h_ref, mod_ref, a_ref, q_ref, z_ref, kc_ref, vc_ref,
         lq1_ref, lk1_ref, lq2_ref, lk2_ref, subw_ref, wout_ref, o_ref) = refs
        kl_ref = vl_ref = None
    row = pl.program_id(0) if ctx_row is None else ctx_row
    tq = q_ref.shape[1]
    vdim = 2 * B_HEAD_DIM

    lam = (jnp.exp(jnp.sum(lq1_ref[...] * lk1_ref[...], axis=-1, keepdims=True))
           - jnp.exp(jnp.sum(lq2_ref[...] * lk2_ref[...], axis=-1, keepdims=True)) + lam_init)

    blocks = [("ctx", 0, kc_ref.shape[1])]
    if with_lat_keys:
        blocks = _key_blocks(kc_ref.shape[1], kl_ref.shape[1], kblock)
    lane = lax.broadcasted_iota(jnp.int32, (tq, vdim), 1)
    first_sub = lane < B_HEAD_DIM
    zero = jnp.zeros((tq, vdim), BF16)

    heads = []
    for hd in range(B_HEADS):
        cs = slice(hd * vdim, (hd + 1) * vdim)
        qh = q_ref[0, :, cs]
        q2 = jnp.concatenate([jnp.where(first_sub, qh, zero), jnp.where(first_sub, zero, qh)], axis=0)
        kv = []
        for src, start, size in blocks:
            kr, vr = (kc_ref, vc_ref) if src == "ctx" else (kl_ref, vl_ref)
            kv.append((kr[0, start:start + size, cs], vr[0, start:start + size, cs]))
        acc, l = _flash(q2, kv)
        o = acc / l
        o = o[:tq] - lam * o[tq:]
        o = _rms(o, subw_ref[...], SUBLN_EPS) * (1.0 - lam_init)
        heads.append(o * z_ref[0, :, cs].astype(F32))
    mix = jnp.concatenate([a_ref[0]] + [hh.astype(BF16) for hh in heads], axis=1)
    out = _dot(mix, wout_ref[...])
    gate = mod_ref[pl.ds(row, 1), 2 * d:3 * d]
    o_ref[0] = h_ref[0] + gate * out


def _even_attn(h, mod, a, q, z, k_ctx, v_ctx, k_lat, v_lat, lparams, subln_w, w_out, *,
               tq, ctx_row, lam_init, kblock):
    bsz, rows, d = h.shape
    bw = q.shape[-1]
    with_lat = k_lat is not None
    const = lambda *_: (0, 0)
    tile = lambda w: pl.BlockSpec((1, tq, w), lambda b, j: (b, j, 0))
    full = lambda arr: pl.BlockSpec((1,) + arr.shape[1:], lambda b, j: (b, 0, 0))
    in_specs = [tile(d), pl.BlockSpec(mod.shape, const), tile(bw), tile(bw), tile(bw),
                full(k_ctx), full(v_ctx)]
    args = [h, mod, a, q, z, k_ctx, v_ctx]
    if with_lat:
        in_specs += [full(k_lat), full(v_lat)]
        args += [k_lat, v_lat]
    in_specs += [pl.BlockSpec((1, B_HEAD_DIM), const)] * 4
    in_specs += [pl.BlockSpec(subln_w.shape, const), pl.BlockSpec(w_out.shape, const)]
    args += list(lparams) + [subln_w, w_out]
    n_keys = k_ctx.shape[1] + (k_lat.shape[1] if with_lat else 0)
    blocks = (2 * _nbytes((tq, d), F32) + 3 * _nbytes((tq, bw), BF16) + 2 * _nbytes((n_keys, bw), BF16)
              + _nbytes(w_out.shape, BF16) + _nbytes(mod.shape, F32))
    return pl.pallas_call(
        functools.partial(_even_attn_kernel, with_lat_keys=with_lat, ctx_row=ctx_row, d=d,
                          lam_init=lam_init, kblock=kblock),
        grid=(bsz, rows // tq),
        in_specs=in_specs,
        out_specs=tile(d),
        out_shape=jax.ShapeDtypeStruct((bsz, rows, d), F32),
        compiler_params=pltpu.CompilerParams(
            dimension_semantics=("arbitrary", "arbitrary"),
            vmem_limit_bytes=_vmem_limit(blocks, 24 << 20)),
        name="even_attn_lat" if with_lat else "even_attn_ctx",
    )(*args)


def _odd_in_kernel(*refs, is_ctx, ctx_row, d, scale):
    if is_ctx:
        (h_ref, mod_ref, normw_ref, win_ref, kvw_ref, wkv_ref, kn_ref, kr_ref, v_ref) = refs
    else:
        (h_ref, mod_ref, normw_ref, win_ref, kvw_ref, wkv_ref, qw_ref, wq_ref,
         cos_ref, slo_ref, shi_ref, kn_ref, kr_ref, v_ref, q_ref, z_ref) = refs
    row = pl.program_id(0) if ctx_row is None else ctx_row
    xl = _modulated_input(h_ref[0], mod_ref, row, normw_ref, d)
    nk = C_HEADS * C_NOPE
    o_kv = C_Q_RANK
    o_kr = C_Q_RANK + C_KV_RANK
    o_z = o_kr + LANES

    ckv = _dot(xl, win_ref[:, o_kv:o_kr])
    kvn = _rms(ckv, kvw_ref[...], NORM_EPS).astype(BF16)
    kn_ref[0] = _dot(kvn, wkv_ref[:, 0:nk]).astype(BF16)
    v_ref[0] = _dot(kvn, wkv_ref[:, nk:2 * nk]).astype(BF16)
    kr = _dot(xl, win_ref[:, o_kr:o_z])
    if not is_ctx:
        cos, slo, shi = cos_ref[...], slo_ref[...], shi_ref[...]
        kr = _rope128(kr, cos, slo, shi)
    kr_ref[0] = jnp.concatenate([kr, pltpu.roll(kr, LANES // 2, 1)], axis=1).astype(BF16)
    if is_ctx:
        return

    cq = _dot(xl, win_ref[:, 0:o_kv])
    qn = _rms(cq, qw_ref[...], NORM_EPS).astype(BF16)
    q_ref[0, :, 0:nk] = (_dot(qn, wq_ref[:, 0:nk]) * scale).astype(BF16)
    nr = C_HEADS * C_ROPE
    qr = _dot(qn, wq_ref[:, nk:nk + nr]) * scale
    q_ref[0, :, nk:nk + nr] = jnp.concatenate(
        [_rope128(qr[:, i * LANES:(i + 1) * LANES], cos, slo, shi) for i in range(nr // LANES)],
        axis=1).astype(BF16)
    z_ref[0] = _silu(_dot(xl, win_ref[:, o_z:o_z + d])).astype(BF16)


def _odd_in(h, mod, norm_w, w_in, kv_norm_w, wkv, q_norm_w, wq, rope, *, tm, ctx_row, scale):
    bsz, rows, d = h.shape
    is_ctx = rope is None
    const = lambda *_: (0, 0)
    nk = C_HEADS * C_NOPE
    nq = nk + C_HEADS * C_ROPE
    in_specs = [
        pl.BlockSpec((1, tm, d), lambda b, j: (b, j, 0)),
        pl.BlockSpec(mod.shape, const),
        pl.BlockSpec((1, d), const),
        pl.BlockSpec(w_in.shape, const),
        pl.BlockSpec(kv_norm_w.shape, const),
        pl.BlockSpec(wkv.shape, const),
    ]
    args = [h, mod, norm_w, w_in, kv_norm_w, wkv]
    tile = lambda w: pl.BlockSpec((1, tm, w), lambda b, j: (b, j, 0))
    sds = lambda w: jax.ShapeDtypeStruct((bsz, rows, w), BF16)
    out_specs = [tile(nk), tile(2 * LANES), tile(nk)]
    out_shape = [sds(nk), sds(2 * LANES), sds(nk)]
    if not is_ctx:
        in_specs += [pl.BlockSpec(q_norm_w.shape, const), pl.BlockSpec(wq.shape, const)]
        in_specs += [pl.BlockSpec((tm, LANES), lambda b, j: (j, 0))] * 3
        args += [q_norm_w, wq] + list(rope)
        out_specs += [tile(nq), tile(d)]
        out_shape += [sds(nq), sds(d)]
    blocks = (_nbytes((tm, d), F32) + _nbytes(w_in.shape, BF16) + _nbytes(wkv.shape, BF16)
              + _nbytes(wq.shape, BF16) + _nbytes((tm, 2 * nk + 2 * LANES + nq + d), BF16)
              + _nbytes(mod.shape, F32) + 3 * _nbytes((tm, LANES), F32))
    return pl.pallas_call(
        functools.partial(_odd_in_kernel, is_ctx=is_ctx, ctx_row=ctx_row, d=d, scale=scale),
        grid=(bsz, rows // tm),
        in_specs=in_specs,
        out_specs=out_specs,
        out_shape=out_shape,
        compiler_params=pltpu.CompilerParams(
            dimension_semantics=("arbitrary", "arbitrary"),
            vmem_limit_bytes=_vmem_limit(blocks, 12 * _nbytes((tm, 2 * nk), F32))),
        name="odd_in_ctx" if is_ctx else "odd_in_lat",
    )(*args)


def _odd_attn_kernel(h_ref, mod_ref, q_ref, z_ref, knc_ref, krc_ref, vc_ref, knl_ref, krl_ref, vl_ref,
                     wout_ref, fw_ref, o_ref, *, d, kblock):
    row = pl.program_id(0)
    nk = C_HEADS * C_NOPE
    blocks = _key_blocks(knc_ref.shape[1], knl_ref.shape[1], kblock)
    heads = []
    for hd in range(C_HEADS):
        cs = slice(hd * C_NOPE, (hd + 1) * C_NOPE)
        pair = nk + (hd // 2) * LANES
        qh = jnp.concatenate([q_ref[0, :, cs], q_ref[0, :, pair:pair + LANES]], axis=1)
        rs = slice((hd % 2) * LANES, (hd % 2 + 1) * LANES)
        kv = []
        for src, start, size in blocks:
            knr, krr, vr = (knc_ref, krc_ref, vc_ref) if src == "ctx" else (knl_ref, krl_ref, vl_ref)
            kb = jnp.concatenate([knr[0, start:start + size, cs], krr[0, start:start + size, rs]], axis=1)
            kv.append((kb, vr[0, start:start + size, cs]))
        acc, l = _flash(qh, kv)
        heads.append((acc / l) * z_ref[0, :, cs].astype(F32))
    mix = jnp.concatenate([hh.astype(BF16) for hh in heads], axis=1)
    out = _dot(mix, wout_ref[...])
    gate = mod_ref[pl.ds(row, 1), 2 * d:3 * d]
    h2 = h_ref[0] + gate * out
    o_ref[0] = _rms(h2, fw_ref[...], NORM_EPS)


def _odd_attn(h, mod, q, z, kn_c, kr_c, v_c, kn_l, kr_l, v_l, w_out, final_w, *, tq, kblock):
    bsz, rows, d = h.shape
    const = lambda *_: (0, 0)
    tile = lambda w: pl.BlockSpec((1, tq, w), lambda b, j: (b, j, 0))
    full = lambda arr: pl.BlockSpec((1,) + arr.shape[1:], lambda b, j: (b, 0, 0))
    kv_arrays = [kn_c, kr_c, v_c, kn_l, kr_l, v_l]
    in_specs = ([tile(d), pl.BlockSpec(mod.shape, const), tile(q.shape[-1]), tile(d)]
                + [full(a) for a in kv_arrays]
                + [pl.BlockSpec(w_out.shape, const), pl.BlockSpec((1, d), const)])
    blocks = (2 * _nbytes((tq, d), F32) + _nbytes((tq, q.shape[-1] + d), BF16)
              + sum(_nbytes(a.shape[1:], BF16) for a in kv_arrays)
              + _nbytes(w_out.shape, BF16) + _nbytes(mod.shape, F32))
    return pl.pallas_call(
        functools.partial(_odd_attn_kernel, d=d, kblock=kblock),
        grid=(bsz, rows // tq),
        in_specs=in_specs,
        out_specs=tile(d),
        out_shape=jax.ShapeDtypeStruct((bsz, rows, d), F32),
        compiler_params=pltpu.CompilerParams(
            dimension_semantics=("arbitrary", "arbitrary"),
            vmem_limit_bytes=_vmem_limit(blocks, 16 << 20)),
        name="odd_attn",
    )(h, mod, q, z, *kv_arrays, w_out, final_w)


def kernel(x, c, ctx, c_ctx, norm_w, ada_w, ada_b, even_w_in, a_ws, a_bs, a_ln_w, a_ln_b, b_lq1, b_lk1,
           b_lq2, b_lk2, b_subln_w, even_w_out, odd_w_in, c_q_norm_w, c_wq_b, c_kv_norm_w, c_wkv_b,
           odd_w_out, final_w):
    bsz, seq, d = x.shape
    clen = ctx.shape[1]
    assert bsz + 1 <= MOD_ROWS and seq % GRID_W == 0 and B_HEAD_DIM == C_ROPE
    ctx_row = bsz

    cond = jnp.concatenate([c, c_ctx[None, :], jnp.zeros((MOD_ROWS - bsz - 1, d), F32)], axis=0)
    mod = _modulation(cond, ada_w, ada_b)
    rope = _rope_tables(seq, C_ROPE)

    aw = a_ln_w.shape[-1]
    w_in0 = even_w_in[0].astype(BF16)
    ws = a_ws[0].astype(BF16)
    ws_pairs = jnp.concatenate([ws[0::2], ws[1::2]], axis=2)
    bias = jnp.repeat(a_bs[0].T, aw // A_GROUPS, axis=1)
    ln_w, ln_b = a_ln_w[0][None, :], a_ln_b[0][None, :]
    lparams = [p[0][None, :] for p in (b_lq1, b_lk1, b_lq2, b_lk2)]
    subln_w = b_subln_w[0][None, :]
    w_out0 = even_w_out[0].astype(BF16)
    lam_init = 0.8 - 0.6 * math.exp(-0.3 * 0)
    nw0 = norm_w[0][None, :]

    a_l, q_l, k_l, v_l, z_l = _even_in(x, mod[0], nw0, w_in0, ws_pairs, bias, ln_w, ln_b, rope,
                                       tm=512, ctx_row=None)
    a_c, q_c, k_c, v_c, z_c = _even_in(ctx, mod[0], nw0, w_in0, ws_pairs, bias, ln_w, ln_b, None,
                                       tm=clen, ctx_row=ctx_row)
    h_lat = _even_attn(x, mod[0], a_l, q_l, z_l, k_c, v_c, k_l, v_l, lparams, subln_w, w_out0,
                       tq=256, ctx_row=None, lam_init=lam_init, kblock=512)
    h_ctx = _even_attn(ctx, mod[0], a_c, q_c, z_c, k_c, v_c, None, None, lparams, subln_w, w_out0,
                       tq=clen, ctx_row=ctx_row, lam_init=lam_init, kblock=512)

    w1 = odd_w_in[0]
    o_kr = C_Q_RANK + C_KV_RANK
    w_in1 = jnp.concatenate([w1[:, :o_kr + C_ROPE], jnp.zeros((d, LANES - C_ROPE), F32),
                             w1[:, o_kr + C_ROPE:]], axis=1).astype(BF16)
    wq = c_wq_b[0].reshape(C_Q_RANK, C_HEADS, C_NOPE + C_ROPE)
    wq = jnp.concatenate([wq[:, :, :C_NOPE].reshape(C_Q_RANK, -1),
                          wq[:, :, C_NOPE:].reshape(C_Q_RANK, -1)], axis=1).astype(BF16)
    wkv = c_wkv_b[0].reshape(C_KV_RANK, C_HEADS, C_NOPE + C_V)
    wkv = jnp.concatenate([wkv[:, :, :C_NOPE].reshape(C_KV_RANK, -1),
                           wkv[:, :, C_NOPE:].reshape(C_KV_RANK, -1)], axis=1).astype(BF16)
    qnw, kvnw = c_q_norm_w[0][None, :], c_kv_norm_w[0][None, :]
    w_out1 = odd_w_out[0].astype(BF16)
    nw1 = norm_w[1][None, :]
    scale = (C_NOPE + C_ROPE) ** -0.5

    kn_l, kr_l, vv_l, qq_l, zz_l = _odd_in(h_lat, mod[1], nw1, w_in1, kvnw, wkv, qnw, wq, rope,
                                           tm=512, ctx_row=None, scale=scale)
    kn_c, kr_c, vv_c = _odd_in(h_ctx, mod[1], nw1, w_in1, kvnw, wkv, qnw, wq, None,
                               tm=clen, ctx_row=ctx_row, scale=scale)
    return _odd_attn(h_lat, mod[1], qq_l, zz_l, kn_c, kr_c, vv_c, kn_l, kr_l, vv_l, w_out1,
                     final_w[None, :], tq=256, kblock=512)
```

```python
import functools
import math

import jax
import jax.numpy as jnp
from jax import lax
from jax.experimental import pallas as pl
from jax.experimental.pallas import tpu as pltpu

F32 = jnp.float32
BF16 = jnp.bfloat16

GRID_W = 64
ROPE_THETA = 10000.0
NORM_EPS = 1e-6
SUBLN_EPS = 1e-5
LN_EPS = 1e-5
A_GROUPS = 8
A_CHUNK = 128
B_HEADS = 4
B_HEAD_DIM = 64
C_HEADS = 8
C_NOPE = 128
C_ROPE = 64
C_V = 128
C_Q_RANK = 256
C_KV_RANK = 128

LANES = 128
V7X_VMEM_LIMIT_CAP = 56 << 20
MOD_ROWS = 24

ROPE_SEG = 32
ROPE_HALF = 16
LOG2E = math.log2(math.e)
ONES_ROWS = 16
BOUND_SLACK = 1.0 + 2.0 ** -10
L_FLOOR = 2.0 ** -64


def _vmem_limit(block_bytes, temp_bytes):
    return int(min(2 * block_bytes + temp_bytes, V7X_VMEM_LIMIT_CAP))


def _nbytes(shape, dtype):
    return math.prod(shape) * jnp.dtype(dtype).itemsize


def _silu(x):
    return x * (1.0 / (1.0 + jnp.exp(-x)))


def _gelu(x):
    return 0.5 * x * (1.0 + lax.erf(x * (2.0 ** -0.5)))


def _rms(x, w, eps):
    return x * lax.rsqrt(jnp.mean(x * x, axis=-1, keepdims=True) + eps) * w


def _dot(a, b):
    return jnp.dot(a, b, preferred_element_type=F32)


def _dot_nt(a, b):
    return lax.dot_general(a, b, (((1,), (1,)), ((), ())), preferred_element_type=F32)


def _rope128(x, cos, sin_lo, sin_hi):
    return (x * cos + pltpu.roll(x, LANES - ROPE_HALF, 1) * sin_lo
            + pltpu.roll(x, ROPE_HALF, 1) * sin_hi)


def _rope_tables(seq, dim):
    rows = seq // GRID_W
    row = jnp.repeat(jnp.arange(rows), GRID_W).astype(F32)
    col = jnp.tile(jnp.arange(GRID_W), rows).astype(F32)
    half = dim // 2
    inv = ROPE_THETA ** (-jnp.arange(0, half, 2, dtype=F32) / half)
    ang_r = row[:, None] * inv[None, :]
    ang_c = col[:, None] * inv[None, :]
    ang = jnp.concatenate([ang_r, ang_r, ang_c, ang_c], axis=-1)
    ang = jnp.tile(ang, (1, LANES // dim))
    first_half = (jnp.arange(LANES) % ROPE_SEG) < ROPE_HALF
    sin = jnp.sin(ang)
    return (jnp.cos(ang), jnp.where(first_half, -sin, 0.0), jnp.where(first_half, 0.0, sin))


def _attend_t(k, q, v_t):
    s = _dot_nt(k, q)
    m = jnp.max(s, axis=0, keepdims=True)
    p = jnp.exp2(s - m)
    l = jnp.sum(p, axis=0, keepdims=True)
    return _dot(v_t, p.astype(BF16)) * (1.0 / l)


def _attend_t_bounded(k, q, v_ext, m):
    s = _dot_nt(k, q)
    p = jnp.exp2(s - m).astype(BF16)
    o = _dot(v_ext, p)
    dv = v_ext.shape[0] - ONES_ROWS
    l = o[dv:dv + 1, :]
    return o[:dv] * (1.0 / l), l


def _with_ones(v_t):
    return jnp.concatenate([v_t, jnp.ones((ONES_ROWS, v_t.shape[1]), BF16)], axis=0)


def _lane_pack(cols):
    rows = cols[0].shape[0]
    lane = lax.broadcasted_iota(jnp.int32, (rows, LANES), 1)
    out = jnp.zeros((rows, LANES), F32)
    for i, c in enumerate(cols):
        out = jnp.where(lane == i, c, out)
    return out


def _sumsq(x_bf16, mask=None):
    xf = x_bf16.astype(F32)
    sq = xf * xf
    if mask is not None:
        sq = jnp.where(mask, sq, 0.0)
    return jnp.sum(sq, axis=-1, keepdims=True)


def _score_bounds(q_sumsq, k_sumsq_ref):
    kmax = jnp.max(k_sumsq_ref[0], axis=0, keepdims=True)
    return (jnp.sqrt(q_sumsq * kmax) * BOUND_SLACK).T


def _mod_kernel(c_ref, w_ref, b_ref, o_ref):
    a = _silu(c_ref[...])
    w = w_ref[0]
    a_hi = a.astype(BF16)
    a_lo = (a - a_hi.astype(F32)).astype(BF16)
    w_hi = w.astype(BF16)
    w_lo = (w - w_hi.astype(F32)).astype(BF16)
    o_ref[0] = _dot(a_hi, w_hi) + (_dot(a_lo, w_hi) + _dot(a_hi, w_lo)) + b_ref[0]


def _modulation(cond, ada_w, ada_b):
    depth, d, n = ada_w.shape
    tn = 768
    return pl.pallas_call(
        _mod_kernel,
        grid=(depth, n // tn),
        in_specs=[
            pl.BlockSpec((MOD_ROWS, d), lambda l, j: (0, 0)),
            pl.BlockSpec((1, d, tn), lambda l, j: (l, 0, j)),
            pl.BlockSpec((1, 1, tn), lambda l, j: (l, 0, j)),
        ],
        out_specs=pl.BlockSpec((1, MOD_ROWS, tn), lambda l, j: (l, 0, j)),
        out_shape=jax.ShapeDtypeStruct((depth, MOD_ROWS, n), F32),
        compiler_params=pltpu.CompilerParams(
            dimension_semantics=("arbitrary", "arbitrary"),
            vmem_limit_bytes=_vmem_limit(_nbytes((d, tn), F32), 16 << 20)),
        name="modulation",
    )(cond, ada_w, ada_b.reshape(depth, 1, n))


def _modulated_input(h, mod_ref, row, normw_ref, d):
    shift = mod_ref[pl.ds(row, 1), 0:d]
    scale = mod_ref[pl.ds(row, 1), d:2 * d]
    xn = _rms(h, normw_ref[...], NORM_EPS)
    return (xn * (1.0 + scale) + shift).astype(BF16)


def _even_in_kernel(*refs, use_rope, n_alias, ctx_row, d, aw):
    refs = refs[:8] + refs[8 + n_alias:]
    if use_rope:
        (h_ref, mod_ref, normw_ref, win_ref, ws_ref, bias_ref, lnw_ref, lnb_ref,
         cos_ref, slo_ref, shi_ref, wvt_ref, a_ref, q_ref, k_ref, vt_ref, z_ref, kss_ref) = refs
    else:
        (h_ref, mod_ref, normw_ref, win_ref, ws_ref, bias_ref, lnw_ref, lnb_ref,
         wvt_ref, a_ref, q_ref, k_ref, vt_ref, z_ref, kss_ref) = refs
    row = pl.program_id(0) if ctx_row is None else ctx_row
    xl = _modulated_input(h_ref[0], mod_ref, row, normw_ref, d)
    tm = xl.shape[0]

    bq = _dot(xl, win_ref[:, 3 * aw:4 * aw]) * (B_HEAD_DIM ** -0.5 * LOG2E)
    bk = _dot(xl, win_ref[:, 4 * aw:5 * aw])
    if use_rope:
        cos, slo, shi = cos_ref[...], slo_ref[...], shi_ref[...]
        bq = jnp.concatenate([_rope128(bq[:, i * LANES:(i + 1) * LANES], cos, slo, shi)
                              for i in range(aw // LANES)], axis=1)
        bk = jnp.concatenate([_rope128(bk[:, i * LANES:(i + 1) * LANES], cos, slo, shi)
                              for i in range(aw // LANES)], axis=1)
    q_ref[0] = bq.astype(BF16)
    kb = bk.astype(BF16)
    k_ref[0] = kb
    low = lax.broadcasted_iota(jnp.int32, (tm, LANES), 1) < B_HEAD_DIM
    kss_ref[0] = _lane_pack([_sumsq(kb[:, (i // 2) * LANES:(i // 2 + 1) * LANES], low if i % 2 == 0 else ~low)
                             for i in range(2 * B_HEADS)])
    vt_ref[0] = _dot_nt(wvt_ref[...], xl).astype(BF16)
    z_ref[0] = _silu(_dot(xl, win_ref[:, 6 * aw:7 * aw])).astype(BF16)

    uvz = _dot(xl, win_ref[:, 0:3 * aw])
    au, av, az = uvz[:, 0:aw], uvz[:, aw:2 * aw], uvz[:, 2 * aw:3 * aw]
    gv = _gelu(av)
    mu = jnp.mean(gv, axis=-1, keepdims=True)
    vc = gv - mu
    vn = (vc * lax.rsqrt(jnp.mean(vc * vc, axis=-1, keepdims=True) + LN_EPS) * lnw_ref[...]
          + lnb_ref[...]).astype(BF16)
    lane = lax.broadcasted_iota(jnp.int32, (A_CHUNK, LANES), 1)
    first_group = lane < (LANES // 2)
    zero = jnp.zeros((A_CHUNK, LANES), BF16)
    chunks = []
    for c in range(tm // A_CHUNK):
        cols = []
        for p in range(aw // LANES):
            vp = vn[c * A_CHUNK:(c + 1) * A_CHUNK, p * LANES:(p + 1) * LANES]
            rhs = jnp.concatenate([jnp.where(first_group, vp, zero), jnp.where(first_group, zero, vp)], axis=0)
            cols.append(_dot(ws_ref[p], rhs))
        chunks.append(jnp.concatenate(cols, axis=1) + bias_ref[...])
    mixed = jnp.concatenate(chunks, axis=0)
    a_ref[0] = (_gelu(au) * mixed * _silu(az)).astype(BF16)


def _even_in(h, mod, norm_w, w_in, ws_pairs, bias, ln_w, ln_b, w_vt, rope, *, tm, ctx_row, total_rows,
             row_offset, alias):
    bsz, rows, d = h.shape
    aw = ln_w.shape[-1]
    n_in = w_in.shape[1]
    use_rope = rope is not None
    off = row_offset // tm
    const = lambda *_: (0, 0)
    in_specs = [
        pl.BlockSpec((1, tm, d), lambda b, j: (b, j, 0)),
        pl.BlockSpec(mod.shape, const),
        pl.BlockSpec((1, d), const),
        pl.BlockSpec((d, n_in), const),
        pl.BlockSpec(ws_pairs.shape, lambda *_: (0, 0, 0)),
        pl.BlockSpec(bias.shape, const),
        pl.BlockSpec((1, aw), const),
        pl.BlockSpec((1, aw), const),
    ]
    args = [h, mod, norm_w, w_in, ws_pairs, bias, ln_w, ln_b]
    aliases = {}
    if alias is not None:
        in_specs += [pl.BlockSpec(memory_space=pl.ANY)] * len(alias)
        aliases = {len(args) + i: i for i in range(len(alias))}
        args += list(alias)
    if use_rope:
        in_specs += [pl.BlockSpec((tm, LANES), lambda b, j: (j, 0))] * 3
        args += list(rope)
    in_specs += [pl.BlockSpec(w_vt.shape, const)]
    args += [w_vt]
    row_spec = pl.BlockSpec((1, tm, aw), lambda b, j: (b, j + off, 0))
    col_spec = pl.BlockSpec((1, aw, tm), lambda b, j: (b, 0, j + off))
    row_sds = jax.ShapeDtypeStruct((bsz, total_rows, aw), BF16)
    col_sds = jax.ShapeDtypeStruct((bsz, aw, total_rows), BF16)
    blocks = (_nbytes((tm, d), F32) + _nbytes((d, n_in + aw), BF16) + 5 * _nbytes((tm, aw), BF16)
              + _nbytes(mod.shape, F32) + 3 * _nbytes((tm, LANES), F32))
    return pl.pallas_call(
        functools.partial(_even_in_kernel, use_rope=use_rope, n_alias=len(aliases), ctx_row=ctx_row, d=d, aw=aw),
        grid=(bsz, rows // tm),
        in_specs=in_specs,
        out_specs=[row_spec, row_spec, row_spec, col_spec, row_spec,
                   pl.BlockSpec((1, tm, LANES), lambda b, j: (b, j + off, 0))],
        out_shape=[row_sds, row_sds, row_sds, col_sds, row_sds,
                   jax.ShapeDtypeStruct((bsz, total_rows, LANES), F32)],
        input_output_aliases=aliases,
        compiler_params=pltpu.CompilerParams(
            dimension_semantics=("arbitrary", "arbitrary"),
            vmem_limit_bytes=_vmem_limit(blocks, 10 * _nbytes((tm, n_in), F32))),
        name="even_in_lat" if use_rope else "even_in_ctx",
    )(*args)


def _even_attn_kernel(h_ref, mod_ref, a_ref, q_ref, z_ref, k_ref, vt_ref, kss_ref, lq1_ref, lk1_ref, lq2_ref,
                      lk2_ref, subw_ref, wout_ref, o_ref, mix_ref, *, ctx_row, d, lam_init, tiles_per_sample):
    t = pl.program_id(0)
    slot = lax.rem(t, 2)
    tq = q_ref.shape[1]
    vdim = 2 * B_HEAD_DIM

    @pl.when(t == 0)
    def _():
        mix_ref[...] = jnp.zeros(mix_ref.shape, mix_ref.dtype)

    row = lax.div(jnp.maximum(t - 1, 0), tiles_per_sample) if ctx_row is None else ctx_row
    out = _dot(jnp.concatenate([a_ref[0], mix_ref[1 - slot]], axis=1), wout_ref[...])
    gate = mod_ref[pl.ds(row, 1), 2 * d:3 * d]
    o_ref[0] = h_ref[0] + gate * out

    lam = (jnp.exp(jnp.sum(lq1_ref[...] * lk1_ref[...], axis=-1, keepdims=True))
           - jnp.exp(jnp.sum(lq2_ref[...] * lk2_ref[...], axis=-1, keepdims=True)) + lam_init)

    lane = lax.broadcasted_iota(jnp.int32, (tq, vdim), 1)
    first_sub = lane < B_HEAD_DIM
    zero = jnp.zeros((tq, vdim), BF16)

    def run(exact):
        if not exact:
            qss = _lane_pack([_sumsq(q_ref[0, :, (i // 2) * vdim:(i // 2 + 1) * vdim],
                                     first_sub if i % 2 == 0 else ~first_sub) for i in range(2 * B_HEADS)])
            bounds = _score_bounds(qss, kss_ref)
        lmin = None
        for hd in range(B_HEADS):
            cs = slice(hd * vdim, (hd + 1) * vdim)
            qh = q_ref[0, :, cs]
            q2 = jnp.concatenate([jnp.where(first_sub, qh, zero), jnp.where(first_sub, zero, qh)], axis=0)
            if exact:
                o_t = _attend_t(k_ref[0, :, cs], q2, vt_ref[0, cs, :])
            else:
                m = jnp.concatenate([bounds[2 * hd:2 * hd + 1], bounds[2 * hd + 1:2 * hd + 2]], axis=1)
                o_t, l = _attend_t_bounded(k_ref[0, :, cs], q2, _with_ones(vt_ref[0, cs, :]), m)
                lmin = l if lmin is None else jnp.minimum(lmin, l)
            o = (o_t[:, :tq] - lam * o_t[:, tq:]).T
            o = _rms(o, subw_ref[...], SUBLN_EPS) * (1.0 - lam_init)
            mix_ref[slot, :, cs] = (o * z_ref[0, :, cs].astype(F32)).astype(BF16)
        return lmin

    lmin = run(exact=False)

    @pl.when(jnp.logical_not(jnp.min(lmin) >= L_FLOOR))
    def _():
        run(exact=True)


def _even_attn(h, mod, a, q, z, k, v_t, kss, lparams, subln_w, w_out, *, tq, ctx_row, lam_init, q_offset,
               key_start, n_keys):
    bsz, rows, d = h.shape
    bw = q.shape[-1]
    nq = rows // tq
    n_tiles = bsz * nq
    const = lambda *_: (0, 0)
    qoff = q_offset // tq
    koff = key_start // n_keys
    cur = lambda t: jnp.minimum(t, n_tiles - 1)
    prev = lambda t: jnp.maximum(t - 1, 0)
    cur_tile = lambda w: pl.BlockSpec((1, tq, w), lambda t: (cur(t) // nq, cur(t) % nq + qoff, 0))
    in_specs = [pl.BlockSpec((1, tq, d), lambda t: (prev(t) // nq, prev(t) % nq, 0)),
                pl.BlockSpec(mod.shape, const),
                pl.BlockSpec((1, tq, bw), lambda t: (prev(t) // nq, prev(t) % nq + qoff, 0)),
                cur_tile(bw), cur_tile(bw),
                pl.BlockSpec((1, n_keys, bw), lambda t: (cur(t) // nq, koff, 0)),
                pl.BlockSpec((1, bw, n_keys), lambda t: (cur(t) // nq, 0, koff)),
                pl.BlockSpec((1, n_keys, LANES), lambda t: (cur(t) // nq, koff, 0))]
    in_specs += [pl.BlockSpec((1, B_HEAD_DIM), const)] * 4
    in_specs += [pl.BlockSpec(subln_w.shape, const), pl.BlockSpec(w_out.shape, const)]
    blocks = (2 * _nbytes((tq, d), F32) + 3 * _nbytes((tq, bw), BF16) + 2 * _nbytes((n_keys, bw), BF16)
              + _nbytes((n_keys, LANES), F32) + _nbytes(w_out.shape, BF16) + _nbytes(mod.shape, F32))
    temps = 3 * _nbytes((n_keys, 2 * tq), F32) + 2 * _nbytes((tq, bw), BF16)
    return pl.pallas_call(
        functools.partial(_even_attn_kernel, ctx_row=ctx_row, d=d, lam_init=lam_init, tiles_per_sample=nq),
        grid=(n_tiles + 1,),
        in_specs=in_specs,
        out_specs=pl.BlockSpec((1, tq, d), lambda t: (prev(t) // nq, prev(t) % nq, 0)),
        out_shape=jax.ShapeDtypeStruct((bsz, rows, d), F32),
        scratch_shapes=[pltpu.VMEM((2, tq, bw), BF16)],
        compiler_params=pltpu.CompilerParams(
            dimension_semantics=("arbitrary",),
            vmem_limit_bytes=_vmem_limit(blocks, temps + (8 << 20))),
        name="even_attn_ctx" if ctx_row is not None else "even_attn_lat",
    )(h, mod, a, q, z, k, v_t, kss, *lparams, subln_w, w_out)


def _odd_in_kernel(*refs, is_ctx, ctx_row, d, scale):
    if is_ctx:
        (h_ref, mod_ref, normw_ref, win_ref, kvw_ref, wk_ref, wvt_ref, _, _, _, _,
         kn_ref, kr_ref, vt_ref, kss_ref) = refs
    else:
        (h_ref, mod_ref, normw_ref, win_ref, kvw_ref, wk_ref, wvt_ref, qw_ref, wq_ref,
         cos_ref, slo_ref, shi_ref, kn_ref, kr_ref, vt_ref, kss_ref, q_ref, z_ref) = refs
    row = pl.program_id(0) if ctx_row is None else ctx_row
    nk = C_HEADS * C_NOPE
    nr = C_HEADS * C_ROPE
    o_kv = C_Q_RANK
    o_kr = C_Q_RANK + C_KV_RANK
    o_z = o_kr + LANES
    xl = _modulated_input(h_ref[0], mod_ref, row, normw_ref, d)
    low_rank = _dot(xl, win_ref[:, 0:o_z])
    ckv = low_rank[:, o_kv:o_kr]
    kvn = _rms(ckv, kvw_ref[...], NORM_EPS).astype(BF16)
    knb = _dot(kvn, wk_ref[...]).astype(BF16)
    kn_ref[0] = knb
    vt_ref[0] = _dot_nt(wvt_ref[...], kvn).astype(BF16)
    kr = low_rank[:, o_kr:o_z]
    if not is_ctx:
        cos, slo, shi = cos_ref[...], slo_ref[...], shi_ref[...]
        kr = _rope128(kr, cos, slo, shi)
    kr_ref[0] = jnp.concatenate([kr, pltpu.roll(kr, LANES // 2, 1)], axis=1).astype(BF16)
    rope_ss = _sumsq(kr.astype(BF16))
    kss_ref[0] = _lane_pack([_sumsq(knb[:, i * C_NOPE:(i + 1) * C_NOPE]) + rope_ss for i in range(C_HEADS)])
    if is_ctx:
        return
    cq = low_rank[:, 0:o_kv]
    qn = _rms(cq, qw_ref[...], NORM_EPS).astype(BF16)
    q_ref[0, :, 0:nk] = (_dot(qn, wq_ref[:, 0:nk]) * scale).astype(BF16)
    qr = _dot(qn, wq_ref[:, nk:nk + nr]) * scale
    q_ref[0, :, nk:nk + nr] = jnp.concatenate(
        [_rope128(qr[:, i * LANES:(i + 1) * LANES], cos, slo, shi) for i in range(nr // LANES)],
        axis=1).astype(BF16)
    z_ref[0] = _silu(_dot(xl, win_ref[:, o_z:o_z + d])).astype(BF16)


def _odd_in(h, mod, norm_w, w_in, kv_norm_w, wk, wv_t, q_norm_w, wq, rope, *, tm, ctx_row, scale, total_rows,
            row_offset, alias):
    bsz, rows, d = h.shape
    is_ctx = rope is None
    const = lambda *_: (0, 0)
    off = row_offset // tm
    nk = C_HEADS * C_NOPE
    nq = nk + C_HEADS * C_ROPE
    in_specs = [
        pl.BlockSpec((1, tm, d), lambda b, j: (b, j, 0)),
        pl.BlockSpec(mod.shape, const),
        pl.BlockSpec((1, d), const),
        pl.BlockSpec(w_in.shape, const),
        pl.BlockSpec(kv_norm_w.shape, const),
        pl.BlockSpec(wk.shape, const),
        pl.BlockSpec(wv_t.shape, const),
    ]
    args = [h, mod, norm_w, w_in, kv_norm_w, wk, wv_t]
    row_spec = lambda w: pl.BlockSpec((1, tm, w), lambda b, j: (b, j + off, 0))
    row_sds = lambda w, r: jax.ShapeDtypeStruct((bsz, r, w), BF16)
    out_specs = [row_spec(nk), row_spec(2 * LANES), pl.BlockSpec((1, nk, tm), lambda b, j: (b, 0, j + off)),
                 row_spec(LANES)]
    out_shape = [row_sds(nk, total_rows), row_sds(2 * LANES, total_rows),
                 jax.ShapeDtypeStruct((bsz, nk, total_rows), BF16),
                 jax.ShapeDtypeStruct((bsz, total_rows, LANES), F32)]
    aliases = {}
    if is_ctx:
        in_specs += [pl.BlockSpec(memory_space=pl.ANY)] * len(alias)
        aliases = {len(args) + i: i for i in range(len(alias))}
        args += list(alias)
    else:
        in_specs += [pl.BlockSpec(q_norm_w.shape, const), pl.BlockSpec(wq.shape, const)]
        in_specs += [pl.BlockSpec((tm, LANES), lambda b, j: (j, 0))] * 3
        args += [q_norm_w, wq] + list(rope)
        tile = lambda w: pl.BlockSpec((1, tm, w), lambda b, j: (b, j, 0))
        out_specs += [tile(nq), tile(d)]
        out_shape += [row_sds(nq, rows), row_sds(d, rows)]
    blocks = (_nbytes((tm, d), F32) + _nbytes(w_in.shape, BF16) + 2 * _nbytes(wk.shape, BF16)
              + _nbytes((C_Q_RANK, nq), BF16) + _nbytes((tm, 2 * nk + 2 * LANES + nq + d), BF16)
              + _nbytes(mod.shape, F32) + 3 * _nbytes((tm, LANES), F32))
    return pl.pallas_call(
        functools.partial(_odd_in_kernel, is_ctx=is_ctx, ctx_row=ctx_row, d=d, scale=scale),
        grid=(bsz, rows // tm),
        in_specs=in_specs,
        out_specs=out_specs,
        out_shape=out_shape,
        input_output_aliases=aliases,
        compiler_params=pltpu.CompilerParams(
            dimension_semantics=("arbitrary", "arbitrary"),
            vmem_limit_bytes=_vmem_limit(blocks, 12 * _nbytes((tm, 2 * nk), F32))),
        name="odd_in_ctx" if is_ctx else "odd_in_lat",
    )(*args)


def _odd_attn_kernel(h_ref, mod_ref, q_ref, z_ref, kn_ref, kr_ref, vt_ref, kss_ref, wout_ref, fw_ref,
                     o_ref, mix_ref, *, d, tiles_per_sample):
    t = pl.program_id(0)
    slot = lax.rem(t, 2)
    nk = C_HEADS * C_NOPE

    @pl.when(t == 0)
    def _():
        mix_ref[...] = jnp.zeros(mix_ref.shape, mix_ref.dtype)

    prev_sample = lax.div(jnp.maximum(t - 1, 0), tiles_per_sample)
    out = _dot(mix_ref[1 - slot], wout_ref[...])
    gate = mod_ref[pl.ds(prev_sample, 1), 2 * d:3 * d]
    h2 = h_ref[0] + gate * out
    o_ref[0] = _rms(h2, fw_ref[...], NORM_EPS)

    def q_head(hd):
        pair = nk + (hd // 2) * LANES
        return q_ref[0, :, hd * C_NOPE:(hd + 1) * C_NOPE], q_ref[0, :, pair:pair + LANES]

    def run(exact):
        if not exact:
            low = lax.broadcasted_iota(jnp.int32, (q_ref.shape[1], LANES), 1) < C_ROPE
            qss = _lane_pack([_sumsq(q_head(hd)[0]) + _sumsq(q_head(hd)[1], low if hd % 2 == 0 else ~low)
                              for hd in range(C_HEADS)])
            bounds = _score_bounds(qss, kss_ref)
        lmin = None
        for hd in range(C_HEADS):
            cs = slice(hd * C_NOPE, (hd + 1) * C_NOPE)
            qh = jnp.concatenate(q_head(hd), axis=1)
            rs = slice((hd % 2) * LANES, (hd % 2 + 1) * LANES)
            kh = jnp.concatenate([kn_ref[0, :, cs], kr_ref[0, :, rs]], axis=1)
            if exact:
                o_t = _attend_t(kh, qh, vt_ref[0, cs, :])
            else:
                o_t, l = _attend_t_bounded(kh, qh, _with_ones(vt_ref[0, cs, :]), bounds[hd:hd + 1])
                lmin = l if lmin is None else jnp.minimum(lmin, l)
            mix_ref[slot, :, cs] = (o_t.T * z_ref[0, :, cs].astype(F32)).astype(BF16)
        return lmin

    lmin = run(exact=False)

    @pl.when(jnp.logical_not(jnp.min(lmin) >= L_FLOOR))
    def _():
        run(exact=True)


def _odd_attn(h, mod, q, z, kn, kr, v_t, kss, w_out, final_w, *, tq):
    bsz, rows, d = h.shape
    nq = rows // tq
    n_tiles = bsz * nq
    const = lambda *_: (0, 0)
    cur = lambda t: jnp.minimum(t, n_tiles - 1)
    prev = lambda t: jnp.maximum(t - 1, 0)
    cur_tile = lambda w: pl.BlockSpec((1, tq, w), lambda t: (cur(t) // nq, cur(t) % nq, 0))
    prev_tile = lambda w: pl.BlockSpec((1, tq, w), lambda t: (prev(t) // nq, prev(t) % nq, 0))
    full = lambda arr: pl.BlockSpec((1,) + arr.shape[1:], lambda t: (cur(t) // nq, 0, 0))
    kv_arrays = [kn, kr, v_t, kss]
    n_keys = kn.shape[1]
    in_specs = ([prev_tile(d), pl.BlockSpec(mod.shape, const), cur_tile(q.shape[-1]), cur_tile(d)]
                + [full(a) for a in kv_arrays]
                + [pl.BlockSpec(w_out.shape, const), pl.BlockSpec((1, d), const)])
    blocks = (2 * _nbytes((tq, d), F32) + _nbytes((tq, q.shape[-1] + d), BF16)
              + sum(_nbytes(a.shape[1:], a.dtype) for a in kv_arrays)
              + _nbytes(w_out.shape, BF16) + _nbytes(mod.shape, F32))
    temps = 4 * _nbytes((n_keys, tq), F32) + 2 * _nbytes((tq, d), BF16)
    return pl.pallas_call(
        functools.partial(_odd_attn_kernel, d=d, tiles_per_sample=nq),
        grid=(n_tiles + 1,),
        in_specs=in_specs,
        out_specs=prev_tile(d),
        out_shape=jax.ShapeDtypeStruct((bsz, rows, d), F32),
        scratch_shapes=[pltpu.VMEM((2, tq, d), BF16)],
        compiler_params=pltpu.CompilerParams(
            dimension_semantics=("arbitrary",),
            vmem_limit_bytes=_vmem_limit(blocks, temps + (8 << 20))),
        name="odd_attn",
    )(h, mod, q, z, *kv_arrays, w_out, final_w)


def kernel(x, c, ctx, c_ctx, norm_w, ada_w, ada_b, even_w_in, a_ws, a_bs, a_ln_w, a_ln_b, b_lq1, b_lk1,
           b_lq2, b_lk2, b_subln_w, even_w_out, odd_w_in, c_q_norm_w, c_wq_b, c_kv_norm_w, c_wkv_b,
           odd_w_out, final_w):
    bsz, seq, d = x.shape
    clen = ctx.shape[1]
    ntok = seq + clen
    assert bsz + 1 <= MOD_ROWS and seq % GRID_W == 0 and B_HEAD_DIM == C_ROPE and seq % clen == 0
    ctx_row = bsz

    cond = jnp.concatenate([c, c_ctx[None, :], jnp.zeros((MOD_ROWS - bsz - 1, d), F32)], axis=0)
    mod = _modulation(cond, ada_w, ada_b)
    rope = _rope_tables(seq, C_ROPE)

    aw = a_ln_w.shape[-1]
    w_in0 = even_w_in[0].astype(BF16)
    w_vt0 = lax.optimization_barrier(w_in0)[:, 5 * aw:6 * aw].T
    ws = a_ws[0].astype(BF16)
    ws_pairs = jnp.concatenate([ws[0::2], ws[1::2]], axis=2)
    bias = jnp.repeat(a_bs[0].T, aw // A_GROUPS, axis=1)
    ln_w, ln_b = a_ln_w[0][None, :], a_ln_b[0][None, :]
    lparams = [p[0][None, :] for p in (b_lq1, b_lk1, b_lq2, b_lk2)]
    subln_w = b_subln_w[0][None, :]
    w_out0 = even_w_out[0].astype(BF16)
    lam_init = 0.8 - 0.6 * math.exp(-0.3 * 0)
    nw0 = norm_w[0][None, :]

    even_lat = _even_in(x, mod[0], nw0, w_in0, ws_pairs, bias, ln_w, ln_b, w_vt0, rope,
                        tm=512, ctx_row=None, total_rows=ntok, row_offset=0, alias=None)
    a_t, q_t, k_t, vt_t, z_t, kss_t = _even_in(ctx, mod[0], nw0, w_in0, ws_pairs, bias, ln_w, ln_b, w_vt0, None,
                                               tm=clen, ctx_row=ctx_row, total_rows=ntok, row_offset=seq,
                                               alias=even_lat)
    h_lat = _even_attn(x, mod[0], a_t, q_t, z_t, k_t, vt_t, kss_t, lparams, subln_w, w_out0,
                       tq=512, ctx_row=None, lam_init=lam_init, q_offset=0, key_start=0, n_keys=ntok)
    h_ctx = _even_attn(ctx, mod[0], a_t, q_t, z_t, k_t, vt_t, kss_t, lparams, subln_w, w_out0,
                       tq=clen, ctx_row=ctx_row, lam_init=lam_init, q_offset=seq, key_start=seq, n_keys=clen)

    w1 = odd_w_in[0]
    o_kr = C_Q_RANK + C_KV_RANK
    w_in1 = jnp.concatenate([w1[:, :o_kr + C_ROPE], jnp.zeros((d, LANES - C_ROPE), F32),
                             w1[:, o_kr + C_ROPE:]], axis=1).astype(BF16)
    wq = c_wq_b[0].reshape(C_Q_RANK, C_HEADS, C_NOPE + C_ROPE)
    wq = jnp.concatenate([wq[:, :, :C_NOPE].reshape(C_Q_RANK, -1),
                          wq[:, :, C_NOPE:].reshape(C_Q_RANK, -1)], axis=1).astype(BF16)
    wkv = c_wkv_b[0].reshape(C_KV_RANK, C_HEADS, C_NOPE + C_V)
    wk = wkv[:, :, :C_NOPE].reshape(C_KV_RANK, -1).astype(BF16)
    wv_t = wkv[:, :, C_NOPE:].reshape(C_KV_RANK, -1).T.astype(BF16)
    qnw, kvnw = c_q_norm_w[0][None, :], c_kv_norm_w[0][None, :]
    w_out1 = odd_w_out[0].astype(BF16)
    nw1 = norm_w[1][None, :]
    scale = (C_NOPE + C_ROPE) ** -0.5 * LOG2E

    kn_l, kr_l, vt_l, kss_l, qq, zz = _odd_in(h_lat, mod[1], nw1, w_in1, kvnw, wk, wv_t, qnw, wq, rope,
                                              tm=512, ctx_row=None, scale=scale, total_rows=ntok, row_offset=0,
                                              alias=None)
    kn, kr, vt, kss = _odd_in(h_ctx, mod[1], nw1, w_in1, kvnw, wk, wv_t, None, None, None,
                              tm=clen, ctx_row=ctx_row, scale=scale, total_rows=ntok, row_offset=seq,
                              alias=(kn_l, kr_l, vt_l, kss_l))
    return _odd_attn(h_lat, mod[1], qq, zz, kn, kr, vt, kss, w_out1, final_w[None, :], tq=512)
```

```python
import functools
import math

import jax
import jax.numpy as jnp
from jax import lax
from jax.experimental import pallas as pl
from jax.experimental.pallas import tpu as pltpu

F32 = jnp.float32
BF16 = jnp.bfloat16

GRID_W = 64
ROPE_THETA = 10000.0
NORM_EPS = 1e-6
SUBLN_EPS = 1e-5
LN_EPS = 1e-5
A_GROUPS = 8
A_CHUNK = 128
B_HEADS = 4
B_HEAD_DIM = 64
C_HEADS = 8
C_NOPE = 128
C_ROPE = 64
C_V = 128
C_Q_RANK = 256
C_KV_RANK = 128

LANES = 128
V7X_VMEM_LIMIT_CAP = 56 << 20
MOD_ROWS = 24

ROPE_SEG = 32
ROPE_HALF = 16
LOG2E = math.log2(math.e)
BOUND_SLACK = 1.0 + 2.0 ** -10
L_FLOOR = 2.0 ** -64


def _vmem_limit(block_bytes, temp_bytes):
    return int(min(2 * block_bytes + temp_bytes, V7X_VMEM_LIMIT_CAP))


def _nbytes(shape, dtype):
    return math.prod(shape) * jnp.dtype(dtype).itemsize


def _silu(x):
    return x * (1.0 / (1.0 + jnp.exp(-x)))


def _gelu(x):
    return 0.5 * x * (1.0 + lax.erf(x * (2.0 ** -0.5)))


def _rms(x, w, eps):
    return x * lax.rsqrt(jnp.mean(x * x, axis=-1, keepdims=True) + eps) * w


def _dot(a, b):
    return jnp.dot(a, b, preferred_element_type=F32)


def _dot_nt(a, b):
    return lax.dot_general(a, b, (((1,), (1,)), ((), ())), preferred_element_type=F32)


def _rope128(x, cos, sin_lo, sin_hi):
    return (x * cos + pltpu.roll(x, LANES - ROPE_HALF, 1) * sin_lo
            + pltpu.roll(x, ROPE_HALF, 1) * sin_hi)


def _rope_tables(seq, dim):
    rows = seq // GRID_W
    row = jnp.repeat(jnp.arange(rows), GRID_W).astype(F32)
    col = jnp.tile(jnp.arange(GRID_W), rows).astype(F32)
    half = dim // 2
    inv = ROPE_THETA ** (-jnp.arange(0, half, 2, dtype=F32) / half)
    ang_r = row[:, None] * inv[None, :]
    ang_c = col[:, None] * inv[None, :]
    ang = jnp.concatenate([ang_r, ang_r, ang_c, ang_c], axis=-1)
    ang = jnp.tile(ang, (1, LANES // dim))
    first_half = (jnp.arange(LANES) % ROPE_SEG) < ROPE_HALF
    sin = jnp.sin(ang)
    return (jnp.cos(ang), jnp.where(first_half, -sin, 0.0), jnp.where(first_half, 0.0, sin))


def _attend_t(k, q, v_t):
    s = _dot_nt(k, q)
    m = jnp.max(s, axis=0, keepdims=True)
    p = jnp.exp2(s - m)
    l = jnp.sum(p, axis=0, keepdims=True)
    return _dot(v_t, p.astype(BF16)) * (1.0 / l)


def _attend_t_bounded(k, q, v_t, m):
    s = _dot_nt(k, q)
    p = jnp.exp2(s - m)
    l = jnp.sum(p, axis=0, keepdims=True)
    return _dot(v_t, p.astype(BF16)) * (1.0 / l), l


def _lane_pack(cols):
    rows = cols[0].shape[0]
    lane = lax.broadcasted_iota(jnp.int32, (rows, LANES), 1)
    out = jnp.zeros((rows, LANES), F32)
    for i, c in enumerate(cols):
        out = jnp.where(lane == i, c, out)
    return out


def _sumsq(x_bf16, mask=None):
    xf = x_bf16.astype(F32)
    sq = xf * xf
    if mask is not None:
        sq = jnp.where(mask, sq, 0.0)
    return jnp.sum(sq, axis=-1, keepdims=True)


def _score_bounds(q_sumsq, k_sumsq_ref):
    kmax = jnp.max(k_sumsq_ref[0], axis=0, keepdims=True)
    return (jnp.sqrt(q_sumsq * kmax) * BOUND_SLACK).T


def _mod_kernel(c_ref, w_ref, b_ref, o_ref):
    a = _silu(c_ref[...])
    w = w_ref[0]
    a_hi = a.astype(BF16)
    a_lo = (a - a_hi.astype(F32)).astype(BF16)
    w_hi = w.astype(BF16)
    w_lo = (w - w_hi.astype(F32)).astype(BF16)
    o_ref[0] = _dot(a_hi, w_hi) + (_dot(a_lo, w_hi) + _dot(a_hi, w_lo)) + b_ref[0]


def _modulation(cond, ada_w, ada_b):
    depth, d, n = ada_w.shape
    tn = 768
    return pl.pallas_call(
        _mod_kernel,
        grid=(depth, n // tn),
        in_specs=[
            pl.BlockSpec((MOD_ROWS, d), lambda l, j: (0, 0)),
            pl.BlockSpec((1, d, tn), lambda l, j: (l, 0, j)),
            pl.BlockSpec((1, 1, tn), lambda l, j: (l, 0, j)),
        ],
        out_specs=pl.BlockSpec((1, MOD_ROWS, tn), lambda l, j: (l, 0, j)),
        out_shape=jax.ShapeDtypeStruct((depth, MOD_ROWS, n), F32),
        compiler_params=pltpu.CompilerParams(
            dimension_semantics=("arbitrary", "arbitrary"),
            vmem_limit_bytes=_vmem_limit(_nbytes((d, tn), F32), 16 << 20)),
        name="modulation",
    )(cond, ada_w, ada_b.reshape(depth, 1, n))


def _transpose_cast_kernel(w_ref, o_ref):
    o_ref[...] = w_ref[0].T.astype(BF16)


def _transposed_columns(w, start, size):
    d = w.shape[1]
    return pl.pallas_call(
        _transpose_cast_kernel,
        grid=(1,),
        in_specs=[pl.BlockSpec((1, d, size), lambda i: (0, 0, start // size))],
        out_specs=pl.BlockSpec((size, d), lambda i: (0, 0)),
        out_shape=jax.ShapeDtypeStruct((size, d), BF16),
        name="transpose_cast",
    )(w)


def _modulated_input(h, mod_ref, row, normw_ref, d):
    shift = mod_ref[pl.ds(row, 1), 0:d]
    scale = mod_ref[pl.ds(row, 1), d:2 * d]
    xn = _rms(h, normw_ref[...], NORM_EPS)
    return (xn * (1.0 + scale) + shift).astype(BF16)


def _even_in_kernel(*refs, use_rope, n_alias, ctx_row, d, aw):
    refs = refs[:8] + refs[8 + n_alias:]
    if use_rope:
        (h_ref, mod_ref, normw_ref, win_ref, ws_ref, bias_ref, lnw_ref, lnb_ref,
         cos_ref, slo_ref, shi_ref, wvt_ref, a_ref, q_ref, k_ref, vt_ref, z_ref, kss_ref) = refs
    else:
        (h_ref, mod_ref, normw_ref, win_ref, ws_ref, bias_ref, lnw_ref, lnb_ref,
         wvt_ref, a_ref, q_ref, k_ref, vt_ref, z_ref, kss_ref) = refs
    row = pl.program_id(0) if ctx_row is None else ctx_row
    xl = _modulated_input(h_ref[0], mod_ref, row, normw_ref, d)
    tm = xl.shape[0]

    bq = _dot(xl, win_ref[:, 3 * aw:4 * aw]) * (B_HEAD_DIM ** -0.5 * LOG2E)
    bk = _dot(xl, win_ref[:, 4 * aw:5 * aw])
    if use_rope:
        cos, slo, shi = cos_ref[...], slo_ref[...], shi_ref[...]
        bq = jnp.concatenate([_rope128(bq[:, i * LANES:(i + 1) * LANES], cos, slo, shi)
                              for i in range(aw // LANES)], axis=1)
        bk = jnp.concatenate([_rope128(bk[:, i * LANES:(i + 1) * LANES], cos, slo, shi)
                              for i in range(aw // LANES)], axis=1)
    q_ref[0] = bq.astype(BF16)
    kb = bk.astype(BF16)
    k_ref[0] = kb
    low = lax.broadcasted_iota(jnp.int32, (tm, LANES), 1) < B_HEAD_DIM
    kss_ref[0] = _lane_pack([_sumsq(kb[:, (i // 2) * LANES:(i // 2 + 1) * LANES], low if i % 2 == 0 else ~low)
                             for i in range(2 * B_HEADS)])
    vt_ref[0] = _dot_nt(wvt_ref[...], xl).astype(BF16)
    z_ref[0] = _silu(_dot(xl, win_ref[:, 6 * aw:7 * aw])).astype(BF16)

    uvz = _dot(xl, win_ref[:, 0:3 * aw])
    au, av, az = uvz[:, 0:aw], uvz[:, aw:2 * aw], uvz[:, 2 * aw:3 * aw]
    gv = _gelu(av)
    mu = jnp.mean(gv, axis=-1, keepdims=True)
    vc = gv - mu
    vn = (vc * lax.rsqrt(jnp.mean(vc * vc, axis=-1, keepdims=True) + LN_EPS) * lnw_ref[...]
          + lnb_ref[...]).astype(BF16)
    lane = lax.broadcasted_iota(jnp.int32, (A_CHUNK, LANES), 1)
    first_group = lane < (LANES // 2)
    zero = jnp.zeros((A_CHUNK, LANES), BF16)
    chunks = []
    for c in range(tm // A_CHUNK):
        cols = []
        for p in range(aw // LANES):
            vp = vn[c * A_CHUNK:(c + 1) * A_CHUNK, p * LANES:(p + 1) * LANES]
            rhs = jnp.concatenate([jnp.where(first_group, vp, zero), jnp.where(first_group, zero, vp)], axis=0)
            cols.append(_dot(ws_ref[p], rhs))
        chunks.append(jnp.concatenate(cols, axis=1) + bias_ref[...])
    mixed = jnp.concatenate(chunks, axis=0)
    a_ref[0] = (_gelu(au) * mixed * _silu(az)).astype(BF16)


def _even_in(h, mod, norm_w, w_in, ws_pairs, bias, ln_w, ln_b, w_vt, rope, *, tm, ctx_row, total_rows,
             row_offset, alias):
    bsz, rows, d = h.shape
    aw = ln_w.shape[-1]
    n_in = w_in.shape[1]
    use_rope = rope is not None
    off = row_offset // tm
    const = lambda *_: (0, 0)
    in_specs = [
        pl.BlockSpec((1, tm, d), lambda b, j: (b, j, 0)),
        pl.BlockSpec(mod.shape, const),
        pl.BlockSpec((1, d), const),
        pl.BlockSpec((d, n_in), const),
        pl.BlockSpec(ws_pairs.shape, lambda *_: (0, 0, 0)),
        pl.BlockSpec(bias.shape, const),
        pl.BlockSpec((1, aw), const),
        pl.BlockSpec((1, aw), const),
    ]
    args = [h, mod, norm_w, w_in, ws_pairs, bias, ln_w, ln_b]
    aliases = {}
    if alias is not None:
        in_specs += [pl.BlockSpec(memory_space=pl.ANY)] * len(alias)
        aliases = {len(args) + i: i for i in range(len(alias))}
        args += list(alias)
    if use_rope:
        in_specs += [pl.BlockSpec((tm, LANES), lambda b, j: (j, 0))] * 3
        args += list(rope)
    in_specs += [pl.BlockSpec(w_vt.shape, const)]
    args += [w_vt]
    row_spec = pl.BlockSpec((1, tm, aw), lambda b, j: (b, j + off, 0))
    col_spec = pl.BlockSpec((1, aw, tm), lambda b, j: (b, 0, j + off))
    row_sds = jax.ShapeDtypeStruct((bsz, total_rows, aw), BF16)
    col_sds = jax.ShapeDtypeStruct((bsz, aw, total_rows), BF16)
    blocks = (_nbytes((tm, d), F32) + _nbytes((d, n_in + aw), BF16) + 5 * _nbytes((tm, aw), BF16)
              + _nbytes(mod.shape, F32) + 3 * _nbytes((tm, LANES), F32))
    return pl.pallas_call(
        functools.partial(_even_in_kernel, use_rope=use_rope, n_alias=len(aliases), ctx_row=ctx_row, d=d, aw=aw),
        grid=(bsz, rows // tm),
        in_specs=in_specs,
        out_specs=[row_spec, row_spec, row_spec, col_spec, row_spec,
                   pl.BlockSpec((1, tm, LANES), lambda b, j: (b, j + off, 0))],
        out_shape=[row_sds, row_sds, row_sds, col_sds, row_sds,
                   jax.ShapeDtypeStruct((bsz, total_rows, LANES), F32)],
        input_output_aliases=aliases,
        compiler_params=pltpu.CompilerParams(
            dimension_semantics=("arbitrary", "arbitrary"),
            vmem_limit_bytes=_vmem_limit(blocks, 10 * _nbytes((tm, n_in), F32))),
        name="even_in_lat" if use_rope else "even_in_ctx",
    )(*args)


def _even_attn_kernel(h_ref, mod_ref, a_ref, q_ref, z_ref, k_ref, vt_ref, kss_ref, lq1_ref, lk1_ref, lq2_ref,
                      lk2_ref, subw_ref, wout_ref, o_ref, mix_ref, *, ctx_row, d, lam_init, tiles_per_sample):
    t = pl.program_id(0)
    slot = lax.rem(t, 2)
    tq = q_ref.shape[1]
    vdim = 2 * B_HEAD_DIM

    @pl.when(t == 0)
    def _():
        mix_ref[...] = jnp.zeros(mix_ref.shape, mix_ref.dtype)

    row = lax.div(jnp.maximum(t - 1, 0), tiles_per_sample) if ctx_row is None else ctx_row
    out = _dot(jnp.concatenate([a_ref[0], mix_ref[1 - slot]], axis=1), wout_ref[...])
    gate = mod_ref[pl.ds(row, 1), 2 * d:3 * d]
    o_ref[0] = h_ref[0] + gate * out

    lam = (jnp.exp(jnp.sum(lq1_ref[...] * lk1_ref[...], axis=-1, keepdims=True))
           - jnp.exp(jnp.sum(lq2_ref[...] * lk2_ref[...], axis=-1, keepdims=True)) + lam_init)

    lane = lax.broadcasted_iota(jnp.int32, (tq, vdim), 1)
    first_sub = lane < B_HEAD_DIM
    zero = jnp.zeros((tq, vdim), BF16)

    def run(exact):
        if not exact:
            qss = _lane_pack([_sumsq(q_ref[0, :, (i // 2) * vdim:(i // 2 + 1) * vdim],
                                     first_sub if i % 2 == 0 else ~first_sub) for i in range(2 * B_HEADS)])
            bounds = _score_bounds(qss, kss_ref)
        lmin = None
        for hd in range(B_HEADS):
            cs = slice(hd * vdim, (hd + 1) * vdim)
            qh = q_ref[0, :, cs]
            q2 = jnp.concatenate([jnp.where(first_sub, qh, zero), jnp.where(first_sub, zero, qh)], axis=0)
            if exact:
                o_t = _attend_t(k_ref[0, :, cs], q2, vt_ref[0, cs, :])
            else:
                m = jnp.concatenate([bounds[2 * hd:2 * hd + 1], bounds[2 * hd + 1:2 * hd + 2]], axis=1)
                o_t, l = _attend_t_bounded(k_ref[0, :, cs], q2, vt_ref[0, cs, :], m)
                lmin = l if lmin is None else jnp.minimum(lmin, l)
            o = (o_t[:, :tq] - lam * o_t[:, tq:]).T
            o = _rms(o, subw_ref[...], SUBLN_EPS) * (1.0 - lam_init)
            mix_ref[slot, :, cs] = (o * z_ref[0, :, cs].astype(F32)).astype(BF16)
        return lmin

    lmin = run(exact=False)

    @pl.when(jnp.logical_not(jnp.min(lmin) >= L_FLOOR))
    def _():
        run(exact=True)


def _even_attn(h, mod, a, q, z, k, v_t, kss, lparams, subln_w, w_out, *, tq, ctx_row, lam_init, q_offset,
               key_start, n_keys):
    bsz, rows, d = h.shape
    bw = q.shape[-1]
    nq = rows // tq
    n_tiles = bsz * nq
    const = lambda *_: (0, 0)
    qoff = q_offset // tq
    koff = key_start // n_keys
    cur = lambda t: jnp.minimum(t, n_tiles - 1)
    prev = lambda t: jnp.maximum(t - 1, 0)
    cur_tile = lambda w: pl.BlockSpec((1, tq, w), lambda t: (cur(t) // nq, cur(t) % nq + qoff, 0))
    in_specs = [pl.BlockSpec((1, tq, d), lambda t: (prev(t) // nq, prev(t) % nq, 0)),
                pl.BlockSpec(mod.shape, const),
                pl.BlockSpec((1, tq, bw), lambda t: (prev(t) // nq, prev(t) % nq + qoff, 0)),
                cur_tile(bw), cur_tile(bw),
                pl.BlockSpec((1, n_keys, bw), lambda t: (cur(t) // nq, koff, 0)),
                pl.BlockSpec((1, bw, n_keys), lambda t: (cur(t) // nq, 0, koff)),
                pl.BlockSpec((1, n_keys, LANES), lambda t: (cur(t) // nq, koff, 0))]
    in_specs += [pl.BlockSpec((1, B_HEAD_DIM), const)] * 4
    in_specs += [pl.BlockSpec(subln_w.shape, const), pl.BlockSpec(w_out.shape, const)]
    blocks = (2 * _nbytes((tq, d), F32) + 3 * _nbytes((tq, bw), BF16) + 2 * _nbytes((n_keys, bw), BF16)
              + _nbytes((n_keys, LANES), F32) + _nbytes(w_out.shape, BF16) + _nbytes(mod.shape, F32))
    temps = 3 * _nbytes((n_keys, 2 * tq), F32) + 2 * _nbytes((tq, bw), BF16)
    return pl.pallas_call(
        functools.partial(_even_attn_kernel, ctx_row=ctx_row, d=d, lam_init=lam_init, tiles_per_sample=nq),
        grid=(n_tiles + 1,),
        in_specs=in_specs,
        out_specs=pl.BlockSpec((1, tq, d), lambda t: (prev(t) // nq, prev(t) % nq, 0)),
        out_shape=jax.ShapeDtypeStruct((bsz, rows, d), F32),
        scratch_shapes=[pltpu.VMEM((2, tq, bw), BF16)],
        compiler_params=pltpu.CompilerParams(
            dimension_semantics=("arbitrary",),
            vmem_limit_bytes=_vmem_limit(blocks, temps + (8 << 20))),
        name="even_attn_ctx" if ctx_row is not None else "even_attn_lat",
    )(h, mod, a, q, z, k, v_t, kss, *lparams, subln_w, w_out)


def _odd_in_kernel(*refs, is_ctx, ctx_row, d, scale):
    if is_ctx:
        (h_ref, mod_ref, normw_ref, win_ref, kvw_ref, wk_ref, wvt_ref, _, _, _, _,
         kn_ref, kr_ref, vt_ref, kss_ref) = refs
    else:
        (h_ref, mod_ref, normw_ref, win_ref, kvw_ref, wk_ref, wvt_ref, qw_ref, wq_ref,
         cos_ref, slo_ref, shi_ref, kn_ref, kr_ref, vt_ref, kss_ref, q_ref, z_ref) = refs
    row = pl.program_id(0) if ctx_row is None else ctx_row
    nk = C_HEADS * C_NOPE
    nr = C_HEADS * C_ROPE
    o_kv = C_Q_RANK
    o_kr = C_Q_RANK + C_KV_RANK
    o_z = o_kr + LANES
    xl = _modulated_input(h_ref[0], mod_ref, row, normw_ref, d)
    low_rank = _dot(xl, win_ref[:, 0:o_z])
    ckv = low_rank[:, o_kv:o_kr]
    kvn = _rms(ckv, kvw_ref[...], NORM_EPS).astype(BF16)
    knb = _dot(kvn, wk_ref[...]).astype(BF16)
    kn_ref[0] = knb
    vt_ref[0] = _dot_nt(wvt_ref[...], kvn).astype(BF16)
    kr = low_rank[:, o_kr:o_z]
    if not is_ctx:
        cos, slo, shi = cos_ref[...], slo_ref[...], shi_ref[...]
        kr = _rope128(kr, cos, slo, shi)
    kr_ref[0] = jnp.concatenate([kr, pltpu.roll(kr, LANES // 2, 1)], axis=1).astype(BF16)
    rope_ss = _sumsq(kr.astype(BF16))
    kss_ref[0] = _lane_pack([_sumsq(knb[:, i * C_NOPE:(i + 1) * C_NOPE]) + rope_ss for i in range(C_HEADS)])
    if is_ctx:
        return
    cq = low_rank[:, 0:o_kv]
    qn = _rms(cq, qw_ref[...], NORM_EPS).astype(BF16)
    q_ref[0, :, 0:nk] = (_dot(qn, wq_ref[:, 0:nk]) * scale).astype(BF16)
    qr = _dot(qn, wq_ref[:, nk:nk + nr]) * scale
    q_ref[0, :, nk:nk + nr] = jnp.concatenate(
        [_rope128(qr[:, i * LANES:(i + 1) * LANES], cos, slo, shi) for i in range(nr // LANES)],
        axis=1).astype(BF16)
    z_ref[0] = _silu(_dot(xl, win_ref[:, o_z:o_z + d])).astype(BF16)


def _odd_in(h, mod, norm_w, w_in, kv_norm_w, wk, wv_t, q_norm_w, wq, rope, *, tm, ctx_row, scale, total_rows,
            row_offset, alias):
    bsz, rows, d = h.shape
    is_ctx = rope is None
    const = lambda *_: (0, 0)
    off = row_offset // tm
    nk = C_HEADS * C_NOPE
    nq = nk + C_HEADS * C_ROPE
    in_specs = [
        pl.BlockSpec((1, tm, d), lambda b, j: (b, j, 0)),
        pl.BlockSpec(mod.shape, const),
        pl.BlockSpec((1, d), const),
        pl.BlockSpec(w_in.shape, const),
        pl.BlockSpec(kv_norm_w.shape, const),
        pl.BlockSpec(wk.shape, const),
        pl.BlockSpec(wv_t.shape, const),
    ]
    args = [h, mod, norm_w, w_in, kv_norm_w, wk, wv_t]
    row_spec = lambda w: pl.BlockSpec((1, tm, w), lambda b, j: (b, j + off, 0))
    row_sds = lambda w, r: jax.ShapeDtypeStruct((bsz, r, w), BF16)
    out_specs = [row_spec(nk), row_spec(2 * LANES), pl.BlockSpec((1, nk, tm), lambda b, j: (b, 0, j + off)),
                 row_spec(LANES)]
    out_shape = [row_sds(nk, total_rows), row_sds(2 * LANES, total_rows),
                 jax.ShapeDtypeStruct((bsz, nk, total_rows), BF16),
                 jax.ShapeDtypeStruct((bsz, total_rows, LANES), F32)]
    aliases = {}
    if is_ctx:
        in_specs += [pl.BlockSpec(memory_space=pl.ANY)] * len(alias)
        aliases = {len(args) + i: i for i in range(len(alias))}
        args += list(alias)
    else:
        in_specs += [pl.BlockSpec(q_norm_w.shape, const), pl.BlockSpec(wq.shape, const)]
        in_specs += [pl.BlockSpec((tm, LANES), lambda b, j: (j, 0))] * 3
        args += [q_norm_w, wq] + list(rope)
        tile = lambda w: pl.BlockSpec((1, tm, w), lambda b, j: (b, j, 0))
        out_specs += [tile(nq), tile(d)]
        out_shape += [row_sds(nq, rows), row_sds(d, rows)]
    blocks = (_nbytes((tm, d), F32) + _nbytes(w_in.shape, BF16) + 2 * _nbytes(wk.shape, BF16)
              + _nbytes((C_Q_RANK, nq), BF16) + _nbytes((tm, 2 * nk + 2 * LANES + nq + d), BF16)
              + _nbytes(mod.shape, F32) + 3 * _nbytes((tm, LANES), F32))
    return pl.pallas_call(
        functools.partial(_odd_in_kernel, is_ctx=is_ctx, ctx_row=ctx_row, d=d, scale=scale),
        grid=(bsz, rows // tm),
        in_specs=in_specs,
        out_specs=out_specs,
        out_shape=out_shape,
        input_output_aliases=aliases,
        compiler_params=pltpu.CompilerParams(
            dimension_semantics=("arbitrary", "arbitrary"),
            vmem_limit_bytes=_vmem_limit(blocks, 12 * _nbytes((tm, 2 * nk), F32))),
        name="odd_in_ctx" if is_ctx else "odd_in_lat",
    )(*args)


def _odd_attn_kernel(h_ref, mod_ref, q_ref, z_ref, kn_ref, kr_ref, vt_ref, kss_ref, wout_ref, fw_ref,
                     o_ref, mix_ref, *, d, tiles_per_sample):
    t = pl.program_id(0)
    slot = lax.rem(t, 2)
    nk = C_HEADS * C_NOPE

    @pl.when(t == 0)
    def _():
        mix_ref[...] = jnp.zeros(mix_ref.shape, mix_ref.dtype)

    prev_sample = lax.div(jnp.maximum(t - 1, 0), tiles_per_sample)
    out = _dot(mix_ref[1 - slot], wout_ref[...])
    gate = mod_ref[pl.ds(prev_sample, 1), 2 * d:3 * d]
    h2 = h_ref[0] + gate * out
    o_ref[0] = _rms(h2, fw_ref[...], NORM_EPS)

    def q_head(hd):
        pair = nk + (hd // 2) * LANES
        return q_ref[0, :, hd * C_NOPE:(hd + 1) * C_NOPE], q_ref[0, :, pair:pair + LANES]

    def run(exact):
        if not exact:
            low = lax.broadcasted_iota(jnp.int32, (q_ref.shape[1], LANES), 1) < C_ROPE
            qss = _lane_pack([_sumsq(q_head(hd)[0]) + _sumsq(q_head(hd)[1], low if hd % 2 == 0 else ~low)
                              for hd in range(C_HEADS)])
            bounds = _score_bounds(qss, kss_ref)
        lmin = None
        for hd in range(C_HEADS):
            cs = slice(hd * C_NOPE, (hd + 1) * C_NOPE)
            qh = jnp.concatenate(q_head(hd), axis=1)
            rs = slice((hd % 2) * LANES, (hd % 2 + 1) * LANES)
            kh = jnp.concatenate([kn_ref[0, :, cs], kr_ref[0, :, rs]], axis=1)
            if exact:
                o_t = _attend_t(kh, qh, vt_ref[0, cs, :])
            else:
                o_t, l = _attend_t_bounded(kh, qh, vt_ref[0, cs, :], bounds[hd:hd + 1])
                lmin = l if lmin is None else jnp.minimum(lmin, l)
            mix_ref[slot, :, cs] = (o_t.T * z_ref[0, :, cs].astype(F32)).astype(BF16)
        return lmin

    lmin = run(exact=False)

    @pl.when(jnp.logical_not(jnp.min(lmin) >= L_FLOOR))
    def _():
        run(exact=True)


def _odd_attn(h, mod, q, z, kn, kr, v_t, kss, w_out, final_w, *, tq):
    bsz, rows, d = h.shape
    nq = rows // tq
    n_tiles = bsz * nq
    const = lambda *_: (0, 0)
    cur = lambda t: jnp.minimum(t, n_tiles - 1)
    prev = lambda t: jnp.maximum(t - 1, 0)
    cur_tile = lambda w: pl.BlockSpec((1, tq, w), lambda t: (cur(t) // nq, cur(t) % nq, 0))
    prev_tile = lambda w: pl.BlockSpec((1, tq, w), lambda t: (prev(t) // nq, prev(t) % nq, 0))
    full = lambda arr: pl.BlockSpec((1,) + arr.shape[1:], lambda t: (cur(t) // nq, 0, 0))
    kv_arrays = [kn, kr, v_t, kss]
    n_keys = kn.shape[1]
    in_specs = ([prev_tile(d), pl.BlockSpec(mod.shape, const), cur_tile(q.shape[-1]), cur_tile(d)]
                + [full(a) for a in kv_arrays]
                + [pl.BlockSpec(w_out.shape, const), pl.BlockSpec((1, d), const)])
    blocks = (2 * _nbytes((tq, d), F32) + _nbytes((tq, q.shape[-1] + d), BF16)
              + sum(_nbytes(a.shape[1:], a.dtype) for a in kv_arrays)
              + _nbytes(w_out.shape, BF16) + _nbytes(mod.shape, F32))
    temps = 4 * _nbytes((n_keys, tq), F32) + 2 * _nbytes((tq, d), BF16)
    return pl.pallas_call(
        functools.partial(_odd_attn_kernel, d=d, tiles_per_sample=nq),
        grid=(n_tiles + 1,),
        in_specs=in_specs,
        out_specs=prev_tile(d),
        out_shape=jax.ShapeDtypeStruct((bsz, rows, d), F32),
        scratch_shapes=[pltpu.VMEM((2, tq, d), BF16)],
        compiler_params=pltpu.CompilerParams(
            dimension_semantics=("arbitrary",),
            vmem_limit_bytes=_vmem_limit(blocks, temps + (8 << 20))),
        name="odd_attn",
    )(h, mod, q, z, *kv_arrays, w_out, final_w)


def kernel(x, c, ctx, c_ctx, norm_w, ada_w, ada_b, even_w_in, a_ws, a_bs, a_ln_w, a_ln_b, b_lq1, b_lk1,
           b_lq2, b_lk2, b_subln_w, even_w_out, odd_w_in, c_q_norm_w, c_wq_b, c_kv_norm_w, c_wkv_b,
           odd_w_out, final_w):
    bsz, seq, d = x.shape
    clen = ctx.shape[1]
    ntok = seq + clen
    assert bsz + 1 <= MOD_ROWS and seq % GRID_W == 0 and B_HEAD_DIM == C_ROPE and seq % clen == 0
    ctx_row = bsz

    cond = jnp.concatenate([c, c_ctx[None, :], jnp.zeros((MOD_ROWS - bsz - 1, d), F32)], axis=0)
    mod = _modulation(cond, ada_w, ada_b)
    rope = _rope_tables(seq, C_ROPE)

    aw = a_ln_w.shape[-1]
    w_in0 = even_w_in[0].astype(BF16)
    w_vt0 = _transposed_columns(even_w_in, 5 * aw, aw)
    ws = a_ws[0].astype(BF16)
    ws_pairs = jnp.concatenate([ws[0::2], ws[1::2]], axis=2)
    bias = jnp.repeat(a_bs[0].T, aw // A_GROUPS, axis=1)
    ln_w, ln_b = a_ln_w[0][None, :], a_ln_b[0][None, :]
    lparams = [p[0][None, :] for p in (b_lq1, b_lk1, b_lq2, b_lk2)]
    subln_w = b_subln_w[0][None, :]
    w_out0 = even_w_out[0].astype(BF16)
    lam_init = 0.8 - 0.6 * math.exp(-0.3 * 0)
    nw0 = norm_w[0][None, :]

    even_lat = _even_in(x, mod[0], nw0, w_in0, ws_pairs, bias, ln_w, ln_b, w_vt0, rope,
                        tm=512, ctx_row=None, total_rows=ntok, row_offset=0, alias=None)
    a_t, q_t, k_t, vt_t, z_t, kss_t = _even_in(ctx, mod[0], nw0, w_in0, ws_pairs, bias, ln_w, ln_b, w_vt0, None,
                                               tm=clen, ctx_row=ctx_row, total_rows=ntok, row_offset=seq,
                                               alias=even_lat)
    h_lat = _even_attn(x, mod[0], a_t, q_t, z_t, k_t, vt_t, kss_t, lparams, subln_w, w_out0,
                       tq=512, ctx_row=None, lam_init=lam_init, q_offset=0, key_start=0, n_keys=ntok)
    h_ctx = _even_attn(ctx, mod[0], a_t, q_t, z_t, k_t, vt_t, kss_t, lparams, subln_w, w_out0,
                       tq=clen, ctx_row=ctx_row, lam_init=lam_init, q_offset=seq, key_start=seq, n_keys=clen)

    w1 = odd_w_in[0]
    o_kr = C_Q_RANK + C_KV_RANK
    w_in1 = jnp.concatenate([w1[:, :o_kr + C_ROPE], jnp.zeros((d, LANES - C_ROPE), F32),
                             w1[:, o_kr + C_ROPE:]], axis=1).astype(BF16)
    wq = c_wq_b[0].reshape(C_Q_RANK, C_HEADS, C_NOPE + C_ROPE)
    wq = jnp.concatenate([wq[:, :, :C_NOPE].reshape(C_Q_RANK, -1),
                          wq[:, :, C_NOPE:].reshape(C_Q_RANK, -1)], axis=1).astype(BF16)
    wkv = c_wkv_b[0].reshape(C_KV_RANK, C_HEADS, C_NOPE + C_V)
    wk = wkv[:, :, :C_NOPE].reshape(C_KV_RANK, -1).astype(BF16)
    wv_t = wkv[:, :, C_NOPE:].reshape(C_KV_RANK, -1).T.astype(BF16)
    qnw, kvnw = c_q_norm_w[0][None, :], c_kv_norm_w[0][None, :]
    w_out1 = odd_w_out[0].astype(BF16)
    nw1 = norm_w[1][None, :]
    scale = (C_NOPE + C_ROPE) ** -0.5 * LOG2E

    kn_l, kr_l, vt_l, kss_l, qq, zz = _odd_in(h_lat, mod[1], nw1, w_in1, kvnw, wk, wv_t, qnw, wq, rope,
                                              tm=512, ctx_row=None, scale=scale, total_rows=ntok, row_offset=0,
                                              alias=None)
    kn, kr, vt, kss = _odd_in(h_ctx, mod[1], nw1, w_in1, kvnw, wk, wv_t, None, None, None,
                              tm=clen, ctx_row=ctx_row, scale=scale, total_rows=ntok, row_offset=seq,
                              alias=(kn_l, kr_l, vt_l, kss_l))
    return _odd_attn(h_lat, mod[1], qq, zz, kn, kr, vt, kss, w_out1, final_w[None, :], tq=512)
```

```python
import functools
import math

import jax
import jax.numpy as jnp
from jax import lax
from jax.experimental import pallas as pl
from jax.experimental.pallas import tpu as pltpu

F32 = jnp.float32
BF16 = jnp.bfloat16

GRID_W = 64
ROPE_THETA = 10000.0
NORM_EPS = 1e-6
SUBLN_EPS = 1e-5
LN_EPS = 1e-5
A_GROUPS = 8
A_CHUNK = 128
B_HEADS = 4
B_HEAD_DIM = 64
C_HEADS = 8
C_NOPE = 128
C_ROPE = 64
C_V = 128
C_Q_RANK = 256
C_KV_RANK = 128

LANES = 128
V7X_VMEM_LIMIT_CAP = 56 << 20
MOD_ROWS = 24

ROPE_SEG = 32
ROPE_HALF = 16
LOG2E = math.log2(math.e)
BOUND_SLACK = 1.0 + 2.0 ** -10
L_FLOOR = 2.0 ** -64


def _vmem_limit(block_bytes, temp_bytes):
    return int(min(2 * block_bytes + temp_bytes, V7X_VMEM_LIMIT_CAP))


def _nbytes(shape, dtype):
    return math.prod(shape) * jnp.dtype(dtype).itemsize


def _silu(x):
    return x * (1.0 / (1.0 + jnp.exp(-x)))


def _gelu(x):
    return 0.5 * x * (1.0 + lax.erf(x * (2.0 ** -0.5)))


def _rms(x, w, eps):
    return x * lax.rsqrt(jnp.mean(x * x, axis=-1, keepdims=True) + eps) * w


def _dot(a, b):
    return jnp.dot(a, b, preferred_element_type=F32)


def _dot_nt(a, b):
    return lax.dot_general(a, b, (((1,), (1,)), ((), ())), preferred_element_type=F32)


def _rope128(x, cos, sin_lo, sin_hi):
    return (x * cos + pltpu.roll(x, LANES - ROPE_HALF, 1) * sin_lo
            + pltpu.roll(x, ROPE_HALF, 1) * sin_hi)


def _rope_tables(seq, dim):
    rows = seq // GRID_W
    row = jnp.repeat(jnp.arange(rows), GRID_W).astype(F32)
    col = jnp.tile(jnp.arange(GRID_W), rows).astype(F32)
    half = dim // 2
    inv = ROPE_THETA ** (-jnp.arange(0, half, 2, dtype=F32) / half)
    ang_r = row[:, None] * inv[None, :]
    ang_c = col[:, None] * inv[None, :]
    ang = jnp.concatenate([ang_r, ang_r, ang_c, ang_c], axis=-1)
    ang = jnp.tile(ang, (1, LANES // dim))
    first_half = (jnp.arange(LANES) % ROPE_SEG) < ROPE_HALF
    sin = jnp.sin(ang)
    return (jnp.cos(ang), jnp.where(first_half, -sin, 0.0), jnp.where(first_half, 0.0, sin))


def _attend_t(k, q, v_t):
    s = _dot_nt(k, q)
    m = jnp.max(s, axis=0, keepdims=True)
    p = jnp.exp2(s - m)
    l = jnp.sum(p, axis=0, keepdims=True)
    return _dot(v_t, p.astype(BF16)) * (1.0 / l)


def _attend_t_bounded(k, q, v_t, m):
    s = _dot_nt(k, q)
    p = jnp.exp2(s - m)
    l = jnp.sum(p, axis=0, keepdims=True)
    return _dot(v_t, p.astype(BF16)) * (1.0 / l), l


def _lane_pack(cols):
    rows = cols[0].shape[0]
    lane = lax.broadcasted_iota(jnp.int32, (rows, LANES), 1)
    out = jnp.zeros((rows, LANES), F32)
    for i, c in enumerate(cols):
        out = jnp.where(lane == i, c, out)
    return out


def _sumsq(x_bf16, mask=None):
    xf = x_bf16.astype(F32)
    sq = xf * xf
    if mask is not None:
        sq = jnp.where(mask, sq, 0.0)
    return jnp.sum(sq, axis=-1, keepdims=True)


def _score_bounds(q_sumsq, k_sumsq_ref):
    kmax = jnp.max(k_sumsq_ref[0], axis=0, keepdims=True)
    return (jnp.sqrt(q_sumsq * kmax) * BOUND_SLACK).T


def _mod_kernel(c_ref, w_ref, b_ref, o_ref):
    a = _silu(c_ref[...])
    w = w_ref[0]
    a_hi = a.astype(BF16)
    a_lo = (a - a_hi.astype(F32)).astype(BF16)
    w_hi = w.astype(BF16)
    w_lo = (w - w_hi.astype(F32)).astype(BF16)
    o_ref[0] = _dot(a_hi, w_hi) + (_dot(a_lo, w_hi) + _dot(a_hi, w_lo)) + b_ref[0]


def _modulation(cond, ada_w, ada_b):
    depth, d, n = ada_w.shape
    tn = 768
    return pl.pallas_call(
        _mod_kernel,
        grid=(depth, n // tn),
        in_specs=[
            pl.BlockSpec((MOD_ROWS, d), lambda l, j: (0, 0)),
            pl.BlockSpec((1, d, tn), lambda l, j: (l, 0, j)),
            pl.BlockSpec((1, 1, tn), lambda l, j: (l, 0, j)),
        ],
        out_specs=pl.BlockSpec((1, MOD_ROWS, tn), lambda l, j: (l, 0, j)),
        out_shape=jax.ShapeDtypeStruct((depth, MOD_ROWS, n), F32),
        compiler_params=pltpu.CompilerParams(
            dimension_semantics=("arbitrary", "arbitrary"),
            vmem_limit_bytes=_vmem_limit(_nbytes((d, tn), F32), 16 << 20)),
        name="modulation",
    )(cond, ada_w, ada_b.reshape(depth, 1, n))


def _transpose_cast_kernel(w_ref, o_ref):
    o_ref[...] = w_ref[0].T.astype(BF16)


def _transposed_columns(w, start, size):
    d = w.shape[1]
    return pl.pallas_call(
        _transpose_cast_kernel,
        grid=(1,),
        in_specs=[pl.BlockSpec((1, d, size), lambda i: (0, 0, start // size))],
        out_specs=pl.BlockSpec((size, d), lambda i: (0, 0)),
        out_shape=jax.ShapeDtypeStruct((size, d), BF16),
        name="transpose_cast",
    )(w)


def _modulated_input(h, mod_ref, row, normw_ref, d):
    shift = mod_ref[pl.ds(row, 1), 0:d]
    scale = mod_ref[pl.ds(row, 1), d:2 * d]
    xn = _rms(h, normw_ref[...], NORM_EPS)
    return (xn * (1.0 + scale) + shift).astype(BF16)


def _even_in_kernel(*refs, use_rope, n_alias, ctx_row, d, aw):
    refs = refs[:8] + refs[8 + n_alias:]
    if use_rope:
        (h_ref, mod_ref, normw_ref, win_ref, ws_ref, bias_ref, lnw_ref, lnb_ref,
         cos_ref, slo_ref, shi_ref, wvt_ref, a_ref, q_ref, k_ref, vt_ref, z_ref, kss_ref) = refs
    else:
        (h_ref, mod_ref, normw_ref, win_ref, ws_ref, bias_ref, lnw_ref, lnb_ref,
         wvt_ref, a_ref, q_ref, k_ref, vt_ref, z_ref, kss_ref) = refs
    row = pl.program_id(0) if ctx_row is None else ctx_row
    xl = _modulated_input(h_ref[0], mod_ref, row, normw_ref, d)
    tm = xl.shape[0]

    bq = _dot(xl, win_ref[:, 3 * aw:4 * aw]) * (B_HEAD_DIM ** -0.5 * LOG2E)
    bk = _dot(xl, win_ref[:, 4 * aw:5 * aw])
    if use_rope:
        cos, slo, shi = cos_ref[...], slo_ref[...], shi_ref[...]
        bq = jnp.concatenate([_rope128(bq[:, i * LANES:(i + 1) * LANES], cos, slo, shi)
                              for i in range(aw // LANES)], axis=1)
        bk = jnp.concatenate([_rope128(bk[:, i * LANES:(i + 1) * LANES], cos, slo, shi)
                              for i in range(aw // LANES)], axis=1)
    q_ref[0] = bq.astype(BF16)
    kb = bk.astype(BF16)
    k_ref[0] = kb
    low = lax.broadcasted_iota(jnp.int32, (tm, LANES), 1) < B_HEAD_DIM
    kss_ref[0] = _lane_pack([_sumsq(kb[:, (i // 2) * LANES:(i // 2 + 1) * LANES], low if i % 2 == 0 else ~low)
                             for i in range(2 * B_HEADS)])
    vt_ref[0] = _dot_nt(wvt_ref[...], xl).astype(BF16)
    z_ref[0] = _silu(_dot(xl, win_ref[:, 6 * aw:7 * aw])).astype(BF16)

    uvz = _dot(xl, win_ref[:, 0:3 * aw])
    au, av, az = uvz[:, 0:aw], uvz[:, aw:2 * aw], uvz[:, 2 * aw:3 * aw]
    gv = _gelu(av)
    mu = jnp.mean(gv, axis=-1, keepdims=True)
    vc = gv - mu
    vn = (vc * lax.rsqrt(jnp.mean(vc * vc, axis=-1, keepdims=True) + LN_EPS) * lnw_ref[...]
          + lnb_ref[...]).astype(BF16)
    lane = lax.broadcasted_iota(jnp.int32, (A_CHUNK, LANES), 1)
    first_group = lane < (LANES // 2)
    zero = jnp.zeros((A_CHUNK, LANES), BF16)
    chunks = []
    for c in range(tm // A_CHUNK):
        cols = []
        for p in range(aw // LANES):
            vp = vn[c * A_CHUNK:(c + 1) * A_CHUNK, p * LANES:(p + 1) * LANES]
            rhs = jnp.concatenate([jnp.where(first_group, vp, zero), jnp.where(first_group, zero, vp)], axis=0)
            cols.append(_dot(ws_ref[p], rhs))
        chunks.append(jnp.concatenate(cols, axis=1) + bias_ref[...])
    mixed = jnp.concatenate(chunks, axis=0)
    a_ref[0] = (_gelu(au) * mixed * _silu(az)).astype(BF16)


def _even_in(h, mod, norm_w, w_in, ws_pairs, bias, ln_w, ln_b, w_vt, rope, *, tm, ctx_row, total_rows,
             row_offset, alias):
    bsz, rows, d = h.shape
    aw = ln_w.shape[-1]
    n_in = w_in.shape[1]
    use_rope = rope is not None
    off = row_offset // tm
    const = lambda *_: (0, 0)
    in_specs = [
        pl.BlockSpec((1, tm, d), lambda b, j: (b, j, 0)),
        pl.BlockSpec(mod.shape, const),
        pl.BlockSpec((1, d), const),
        pl.BlockSpec((d, n_in), const),
        pl.BlockSpec(ws_pairs.shape, lambda *_: (0, 0, 0)),
        pl.BlockSpec(bias.shape, const),
        pl.BlockSpec((1, aw), const),
        pl.BlockSpec((1, aw), const),
    ]
    args = [h, mod, norm_w, w_in, ws_pairs, bias, ln_w, ln_b]
    aliases = {}
    if alias is not None:
        in_specs += [pl.BlockSpec(memory_space=pl.ANY)] * len(alias)
        aliases = {len(args) + i: i for i in range(len(alias))}
        args += list(alias)
    if use_rope:
        in_specs += [pl.BlockSpec((tm, LANES), lambda b, j: (j, 0))] * 3
        args += list(rope)
    in_specs += [pl.BlockSpec(w_vt.shape, const)]
    args += [w_vt]
    row_spec = pl.BlockSpec((1, tm, aw), lambda b, j: (b, j + off, 0))
    col_spec = pl.BlockSpec((1, aw, tm), lambda b, j: (b, 0, j + off))
    row_sds = jax.ShapeDtypeStruct((bsz, total_rows, aw), BF16)
    col_sds = jax.ShapeDtypeStruct((bsz, aw, total_rows), BF16)
    blocks = (_nbytes((tm, d), F32) + _nbytes((d, n_in + aw), BF16) + 5 * _nbytes((tm, aw), BF16)
              + _nbytes(mod.shape, F32) + 3 * _nbytes((tm, LANES), F32))
    return pl.pallas_call(
        functools.partial(_even_in_kernel, use_rope=use_rope, n_alias=len(aliases), ctx_row=ctx_row, d=d, aw=aw),
        grid=(bsz, rows // tm),
        in_specs=in_specs,
        out_specs=[row_spec, row_spec, row_spec, col_spec, row_spec,
                   pl.BlockSpec((1, tm, LANES), lambda b, j: (b, j + off, 0))],
        out_shape=[row_sds, row_sds, row_sds, col_sds, row_sds,
                   jax.ShapeDtypeStruct((bsz, total_rows, LANES), F32)],
        input_output_aliases=aliases,
        compiler_params=pltpu.CompilerParams(
            dimension_semantics=("arbitrary", "arbitrary"),
            vmem_limit_bytes=_vmem_limit(blocks, 10 * _nbytes((tm, n_in), F32))),
        name="even_in_lat" if use_rope else "even_in_ctx",
    )(*args)


def _even_attn_kernel(h_ref, mod_ref, a_ref, q_ref, z_ref, k_ref, vt_ref, kss_ref, lq1_ref, lk1_ref, lq2_ref,
                      lk2_ref, subw_ref, wout_ref, o_ref, mix_ref, *, ctx_row, d, lam_init, tiles_per_sample):
    t = pl.program_id(0)
    slot = lax.rem(t, 2)
    tq = q_ref.shape[1]
    vdim = 2 * B_HEAD_DIM

    @pl.when(t == 0)
    def _():
        mix_ref[...] = jnp.zeros(mix_ref.shape, mix_ref.dtype)

    row = lax.div(jnp.maximum(t - 1, 0), tiles_per_sample) if ctx_row is None else ctx_row
    out = _dot(jnp.concatenate([a_ref[0], mix_ref[1 - slot]], axis=1), wout_ref[...])
    gate = mod_ref[pl.ds(row, 1), 2 * d:3 * d]
    o_ref[0] = h_ref[0] + gate * out

    lam = (jnp.exp(jnp.sum(lq1_ref[...] * lk1_ref[...], axis=-1, keepdims=True))
           - jnp.exp(jnp.sum(lq2_ref[...] * lk2_ref[...], axis=-1, keepdims=True)) + lam_init)

    lane = lax.broadcasted_iota(jnp.int32, (tq, vdim), 1)
    first_sub = lane < B_HEAD_DIM
    zero = jnp.zeros((tq, vdim), BF16)

    def run(exact):
        if not exact:
            qss = _lane_pack([_sumsq(q_ref[0, :, (i // 2) * vdim:(i // 2 + 1) * vdim],
                                     first_sub if i % 2 == 0 else ~first_sub) for i in range(2 * B_HEADS)])
            bounds = _score_bounds(qss, kss_ref)
        lmin = None
        for hd in range(B_HEADS):
            cs = slice(hd * vdim, (hd + 1) * vdim)
            qh = q_ref[0, :, cs]
            q2 = jnp.concatenate([jnp.where(first_sub, qh, zero), jnp.where(first_sub, zero, qh)], axis=0)
            if exact:
                o_t = _attend_t(k_ref[0, :, cs], q2, vt_ref[0, cs, :])
            else:
                m = jnp.concatenate([bounds[2 * hd:2 * hd + 1], bounds[2 * hd + 1:2 * hd + 2]], axis=1)
                o_t, l = _attend_t_bounded(k_ref[0, :, cs], q2, vt_ref[0, cs, :], m)
                lmin = l if lmin is None else jnp.minimum(lmin, l)
            o = (o_t[:, :tq] - lam * o_t[:, tq:]).T
            o = _rms(o, subw_ref[...], SUBLN_EPS) * (1.0 - lam_init)
            mix_ref[slot, :, cs] = (o * z_ref[0, :, cs].astype(F32)).astype(BF16)
        return lmin

    lmin = run(exact=False)

    @pl.when(jnp.logical_not(jnp.min(lmin) >= L_FLOOR))
    def _():
        run(exact=True)


def _even_attn(h, mod, a, q, z, k, v_t, kss, lparams, subln_w, w_out, *, tq, ctx_row, lam_init, q_offset,
               key_start, n_keys):
    bsz, rows, d = h.shape
    bw = q.shape[-1]
    nq = rows // tq
    n_tiles = bsz * nq
    const = lambda *_: (0, 0)
    qoff = q_offset // tq
    koff = key_start // n_keys
    cur = lambda t: jnp.minimum(t, n_tiles - 1)
    prev = lambda t: jnp.maximum(t - 1, 0)
    cur_tile = lambda w: pl.BlockSpec((1, tq, w), lambda t: (cur(t) // nq, cur(t) % nq + qoff, 0))
    in_specs = [pl.BlockSpec((1, tq, d), lambda t: (prev(t) // nq, prev(t) % nq, 0)),
                pl.BlockSpec(mod.shape, const),
                pl.BlockSpec((1, tq, bw), lambda t: (prev(t) // nq, prev(t) % nq + qoff, 0)),
                cur_tile(bw), cur_tile(bw),
                pl.BlockSpec((1, n_keys, bw), lambda t: (cur(t) // nq, koff, 0)),
                pl.BlockSpec((1, bw, n_keys), lambda t: (cur(t) // nq, 0, koff)),
                pl.BlockSpec((1, n_keys, LANES), lambda t: (cur(t) // nq, koff, 0))]
    in_specs += [pl.BlockSpec((1, B_HEAD_DIM), const)] * 4
    in_specs += [pl.BlockSpec(subln_w.shape, const), pl.BlockSpec(w_out.shape, const)]
    blocks = (2 * _nbytes((tq, d), F32) + 3 * _nbytes((tq, bw), BF16) + 2 * _nbytes((n_keys, bw), BF16)
              + _nbytes((n_keys, LANES), F32) + _nbytes(w_out.shape, BF16) + _nbytes(mod.shape, F32))
    temps = 3 * _nbytes((n_keys, 2 * tq), F32) + 2 * _nbytes((tq, bw), BF16)
    return pl.pallas_call(
        functools.partial(_even_attn_kernel, ctx_row=ctx_row, d=d, lam_init=lam_init, tiles_per_sample=nq),
        grid=(n_tiles + 1,),
        in_specs=in_specs,
        out_specs=pl.BlockSpec((1, tq, d), lambda t: (prev(t) // nq, prev(t) % nq, 0)),
        out_shape=jax.ShapeDtypeStruct((bsz, rows, d), F32),
        scratch_shapes=[pltpu.VMEM((2, tq, bw), BF16)],
        compiler_params=pltpu.CompilerParams(
            dimension_semantics=("arbitrary",),
            vmem_limit_bytes=_vmem_limit(blocks, temps + (8 << 20))),
        name="even_attn_ctx" if ctx_row is not None else "even_attn_lat",
    )(h, mod, a, q, z, k, v_t, kss, *lparams, subln_w, w_out)


def _odd_in_kernel(*refs, is_ctx, ctx_row, d, scale):
    if is_ctx:
        (h_ref, mod_ref, normw_ref, win_ref, kvw_ref, wk_ref, wvt_ref, _, _, _, _,
         kn_ref, kr_ref, vt_ref, kss_ref) = refs
    else:
        (h_ref, mod_ref, normw_ref, win_ref, kvw_ref, wk_ref, wvt_ref, qw_ref, wq_ref,
         cos_ref, slo_ref, shi_ref, kn_ref, kr_ref, vt_ref, kss_ref, q_ref, z_ref) = refs
    row = pl.program_id(0) if ctx_row is None else ctx_row
    nk = C_HEADS * C_NOPE
    nr = C_HEADS * C_ROPE
    o_kv = C_Q_RANK
    o_kr = C_Q_RANK + C_KV_RANK
    o_z = o_kr + LANES
    xl = _modulated_input(h_ref[0], mod_ref, row, normw_ref, d)
    low_rank = _dot(xl, win_ref[:, 0:o_z])
    ckv = low_rank[:, o_kv:o_kr]
    kvn = _rms(ckv, kvw_ref[...], NORM_EPS).astype(BF16)
    knb = _dot(kvn, wk_ref[...]).astype(BF16)
    kn_ref[0] = knb
    vt_ref[0] = _dot_nt(wvt_ref[...], kvn).astype(BF16)
    kr = low_rank[:, o_kr:o_z]
    if not is_ctx:
        cos, slo, shi = cos_ref[...], slo_ref[...], shi_ref[...]
        kr = _rope128(kr, cos, slo, shi)
    kr_ref[0] = jnp.concatenate([kr, pltpu.roll(kr, LANES // 2, 1)], axis=1).astype(BF16)
    rope_ss = _sumsq(kr.astype(BF16))
    kss_ref[0] = _lane_pack([_sumsq(knb[:, i * C_NOPE:(i + 1) * C_NOPE]) + rope_ss for i in range(C_HEADS)])
    if is_ctx:
        return
    cq = low_rank[:, 0:o_kv]
    qn = _rms(cq, qw_ref[...], NORM_EPS).astype(BF16)
    q_ref[0, :, 0:nk] = (_dot(qn, wq_ref[:, 0:nk]) * scale).astype(BF16)
    qr = _dot(qn, wq_ref[:, nk:nk + nr]) * scale
    q_ref[0, :, nk:nk + nr] = jnp.concatenate(
        [_rope128(qr[:, i * LANES:(i + 1) * LANES], cos, slo, shi) for i in range(nr // LANES)],
        axis=1).astype(BF16)
    z_ref[0] = _silu(_dot(xl, win_ref[:, o_z:o_z + d])).astype(BF16)


def _odd_in(h, mod, norm_w, w_in, kv_norm_w, wk, wv_t, q_norm_w, wq, rope, *, tm, ctx_row, scale, total_rows,
            row_offset, alias):
    bsz, rows, d = h.shape
    is_ctx = rope is None
    const = lambda *_: (0, 0)
    off = row_offset // tm
    nk = C_HEADS * C_NOPE
    nq = nk + C_HEADS * C_ROPE
    in_specs = [
        pl.BlockSpec((1, tm, d), lambda b, j: (b, j, 0)),
        pl.BlockSpec(mod.shape, const),
        pl.BlockSpec((1, d), const),
        pl.BlockSpec(w_in.shape, const),
        pl.BlockSpec(kv_norm_w.shape, const),
        pl.BlockSpec(wk.shape, const),
        pl.BlockSpec(wv_t.shape, const),
    ]
    args = [h, mod, norm_w, w_in, kv_norm_w, wk, wv_t]
    row_spec = lambda w: pl.BlockSpec((1, tm, w), lambda b, j: (b, j + off, 0))
    row_sds = lambda w, r: jax.ShapeDtypeStruct((bsz, r, w), BF16)
    out_specs = [row_spec(nk), row_spec(2 * LANES), pl.BlockSpec((1, nk, tm), lambda b, j: (b, 0, j + off)),
                 row_spec(LANES)]
    out_shape = [row_sds(nk, total_rows), row_sds(2 * LANES, total_rows),
                 jax.ShapeDtypeStruct((bsz, nk, total_rows), BF16),
                 jax.ShapeDtypeStruct((bsz, total_rows, LANES), F32)]
    aliases = {}
    if is_ctx:
        in_specs += [pl.BlockSpec(memory_space=pl.ANY)] * len(alias)
        aliases = {len(args) + i: i for i in range(len(alias))}
        args += list(alias)
    else:
        in_specs += [pl.BlockSpec(q_norm_w.shape, const), pl.BlockSpec(wq.shape, const)]
        in_specs += [pl.BlockSpec((tm, LANES), lambda b, j: (j, 0))] * 3
        args += [q_norm_w, wq] + list(rope)
        tile = lambda w: pl.BlockSpec((1, tm, w), lambda b, j: (b, j, 0))
        out_specs += [tile(nq), tile(d)]
        out_shape += [row_sds(nq, rows), row_sds(d, rows)]
    blocks = (_nbytes((tm, d), F32) + _nbytes(w_in.shape, BF16) + 2 * _nbytes(wk.shape, BF16)
              + _nbytes((C_Q_RANK, nq), BF16) + _nbytes((tm, 2 * nk + 2 * LANES + nq + d), BF16)
              + _nbytes(mod.shape, F32) + 3 * _nbytes((tm, LANES), F32))
    return pl.pallas_call(
        functools.partial(_odd_in_kernel, is_ctx=is_ctx, ctx_row=ctx_row, d=d, scale=scale),
        grid=(bsz, rows // tm),
        in_specs=in_specs,
        out_specs=out_specs,
        out_shape=out_shape,
        input_output_aliases=aliases,
        compiler_params=pltpu.CompilerParams(
            dimension_semantics=("arbitrary", "arbitrary"),
            vmem_limit_bytes=_vmem_limit(blocks, 12 * _nbytes((tm, 2 * nk), F32))),
        name="odd_in_ctx" if is_ctx else "odd_in_lat",
    )(*args)


def _odd_attn_kernel(h_ref, mod_ref, q_ref, z_ref, kn_ref, kr_ref, vt_ref, kss_ref, wout_ref, fw_ref,
                     o_ref, mix_ref, *, d, tiles_per_sample):
    t = pl.program_id(0)
    slot = lax.rem(t, 2)
    nk = C_HEADS * C_NOPE

    @pl.when(t == 0)
    def _():
        mix_ref[...] = jnp.zeros(mix_ref.shape, mix_ref.dtype)

    prev_sample = lax.div(jnp.maximum(t - 1, 0), tiles_per_sample)
    out = _dot(mix_ref[1 - slot], wout_ref[...])
    gate = mod_ref[pl.ds(prev_sample, 1), 2 * d:3 * d]
    h2 = h_ref[0] + gate * out
    o_ref[0] = _rms(h2, fw_ref[...], NORM_EPS)

    def q_head(hd):
        pair = nk + (hd // 2) * LANES
        return q_ref[0, :, hd * C_NOPE:(hd + 1) * C_NOPE], q_ref[0, :, pair:pair + LANES]

    def run(exact):
        if not exact:
            low = lax.broadcasted_iota(jnp.int32, (q_ref.shape[1], LANES), 1) < C_ROPE
            qss = _lane_pack([_sumsq(q_head(hd)[0]) + _sumsq(q_head(hd)[1], low if hd % 2 == 0 else ~low)
                              for hd in range(C_HEADS)])
            bounds = _score_bounds(qss, kss_ref)
        lmin = None
        for hd in range(C_HEADS):
            cs = slice(hd * C_NOPE, (hd + 1) * C_NOPE)
            qh = jnp.concatenate(q_head(hd), axis=1)
            rs = slice((hd % 2) * LANES, (hd % 2 + 1) * LANES)
            kh = jnp.concatenate([kn_ref[0, :, cs], kr_ref[0, :, rs]], axis=1)
            if exact:
                o_t = _attend_t(kh, qh, vt_ref[0, cs, :])
            else:
                o_t, l = _attend_t_bounded(kh, qh, vt_ref[0, cs, :], bounds[hd:hd + 1])
                lmin = l if lmin is None else jnp.minimum(lmin, l)
            mix_ref[slot, :, cs] = (o_t.T * z_ref[0, :, cs].astype(F32)).astype(BF16)
        return lmin

    lmin = run(exact=False)

    @pl.when(jnp.logical_not(jnp.min(lmin) >= L_FLOOR))
    def _():
        run(exact=True)


def _odd_attn(h, mod, q, z, kn, kr, v_t, kss, w_out, final_w, *, tq):
    bsz, rows, d = h.shape
    nq = rows // tq
    n_tiles = bsz * nq
    const = lambda *_: (0, 0)
    cur = lambda t: jnp.minimum(t, n_tiles - 1)
    prev = lambda t: jnp.maximum(t - 1, 0)
    cur_tile = lambda w: pl.BlockSpec((1, tq, w), lambda t: (cur(t) // nq, cur(t) % nq, 0))
    prev_tile = lambda w: pl.BlockSpec((1, tq, w), lambda t: (prev(t) // nq, prev(t) % nq, 0))
    full = lambda arr: pl.BlockSpec((1,) + arr.shape[1:], lambda t: (cur(t) // nq, 0, 0))
    kv_arrays = [kn, kr, v_t, kss]
    n_keys = kn.shape[1]
    in_specs = ([prev_tile(d), pl.BlockSpec(mod.shape, const), cur_tile(q.shape[-1]), cur_tile(d)]
                + [full(a) for a in kv_arrays]
                + [pl.BlockSpec(w_out.shape, const), pl.BlockSpec((1, d), const)])
    blocks = (2 * _nbytes((tq, d), F32) + _nbytes((tq, q.shape[-1] + d), BF16)
              + sum(_nbytes(a.shape[1:], a.dtype) for a in kv_arrays)
              + _nbytes(w_out.shape, BF16) + _nbytes(mod.shape, F32))
    temps = 4 * _nbytes((n_keys, tq), F32) + 2 * _nbytes((tq, d), BF16)
    return pl.pallas_call(
        functools.partial(_odd_attn_kernel, d=d, tiles_per_sample=nq),
        grid=(n_tiles + 1,),
        in_specs=in_specs,
        out_specs=prev_tile(d),
        out_shape=jax.ShapeDtypeStruct((bsz, rows, d), F32),
        scratch_shapes=[pltpu.VMEM((2, tq, d), BF16)],
        compiler_params=pltpu.CompilerParams(
            dimension_semantics=("arbitrary",),
            vmem_limit_bytes=_vmem_limit(blocks, temps + (8 << 20))),
        name="odd_attn",
    )(h, mod, q, z, *kv_arrays, w_out, final_w)


def kernel(x, c, ctx, c_ctx, norm_w, ada_w, ada_b, even_w_in, a_ws, a_bs, a_ln_w, a_ln_b, b_lq1, b_lk1,
           b_lq2, b_lk2, b_subln_w, even_w_out, odd_w_in, c_q_norm_w, c_wq_b, c_kv_norm_w, c_wkv_b,
           odd_w_out, final_w):
    bsz, seq, d = x.shape
    clen = ctx.shape[1]
    ntok = seq + clen
    assert bsz + 1 <= MOD_ROWS and seq % GRID_W == 0 and B_HEAD_DIM == C_ROPE and seq % clen == 0
    ctx_row = bsz

    cond = jnp.concatenate([c, c_ctx[None, :], jnp.zeros((MOD_ROWS - bsz - 1, d), F32)], axis=0)
    mod = _modulation(cond, ada_w, ada_b)
    rope = _rope_tables(seq, C_ROPE)

    aw = a_ln_w.shape[-1]
    w_in0 = even_w_in[0].astype(BF16)
    w_vt0 = _transposed_columns(even_w_in, 5 * aw, aw)
    ws = a_ws[0].astype(BF16)
    ws_pairs = jnp.concatenate([ws[0::2], ws[1::2]], axis=2)
    bias = jnp.repeat(a_bs[0].T, aw // A_GROUPS, axis=1)
    ln_w, ln_b = a_ln_w[0][None, :], a_ln_b[0][None, :]
    lparams = [p[0][None, :] for p in (b_lq1, b_lk1, b_lq2, b_lk2)]
    subln_w = b_subln_w[0][None, :]
    w_out0 = even_w_out[0].astype(BF16)
    lam_init = 0.8 - 0.6 * math.exp(-0.3 * 0)
    nw0 = norm_w[0][None, :]

    even_lat = _even_in(x, mod[0], nw0, w_in0, ws_pairs, bias, ln_w, ln_b, w_vt0, rope,
                        tm=1024, ctx_row=None, total_rows=ntok, row_offset=0, alias=None)
    a_t, q_t, k_t, vt_t, z_t, kss_t = _even_in(ctx, mod[0], nw0, w_in0, ws_pairs, bias, ln_w, ln_b, w_vt0, None,
                                               tm=clen, ctx_row=ctx_row, total_rows=ntok, row_offset=seq,
                                               alias=even_lat)
    h_lat = _even_attn(x, mod[0], a_t, q_t, z_t, k_t, vt_t, kss_t, lparams, subln_w, w_out0,
                       tq=512, ctx_row=None, lam_init=lam_init, q_offset=0, key_start=0, n_keys=ntok)
    h_ctx = _even_attn(ctx, mod[0], a_t, q_t, z_t, k_t, vt_t, kss_t, lparams, subln_w, w_out0,
                       tq=clen, ctx_row=ctx_row, lam_init=lam_init, q_offset=seq, key_start=seq, n_keys=clen)

    w1 = odd_w_in[0]
    o_kr = C_Q_RANK + C_KV_RANK
    w_in1 = jnp.concatenate([w1[:, :o_kr + C_ROPE], jnp.zeros((d, LANES - C_ROPE), F32),
                             w1[:, o_kr + C_ROPE:]], axis=1).astype(BF16)
    wq = c_wq_b[0].reshape(C_Q_RANK, C_HEADS, C_NOPE + C_ROPE)
    wq = jnp.concatenate([wq[:, :, :C_NOPE].reshape(C_Q_RANK, -1),
                          wq[:, :, C_NOPE:].reshape(C_Q_RANK, -1)], axis=1).astype(BF16)
    wkv = c_wkv_b[0].reshape(C_KV_RANK, C_HEADS, C_NOPE + C_V)
    wk = wkv[:, :, :C_NOPE].reshape(C_KV_RANK, -1).astype(BF16)
    wv_t = wkv[:, :, C_NOPE:].reshape(C_KV_RANK, -1).T.astype(BF16)
    qnw, kvnw = c_q_norm_w[0][None, :], c_kv_norm_w[0][None, :]
    w_out1 = odd_w_out[0].astype(BF16)
    nw1 = norm_w[1][None, :]
    scale = (C_NOPE + C_ROPE) ** -0.5 * LOG2E

    kn_l, kr_l, vt_l, kss_l, qq, zz = _odd_in(h_lat, mod[1], nw1, w_in1, kvnw, wk, wv_t, qnw, wq, rope,
                                              tm=1024, ctx_row=None, scale=scale, total_rows=ntok, row_offset=0,
                                              alias=None)
    kn, kr, vt, kss = _odd_in(h_ctx, mod[1], nw1, w_in1, kvnw, wk, wv_t, None, None, None,
                              tm=clen, ctx_row=ctx_row, scale=scale, total_rows=ntok, row_offset=seq,
                              alias=(kn_l, kr_l, vt_l, kss_l))
    return _odd_attn(h_lat, mod[1], qq, zz, kn, kr, vt, kss, w_out1, final_w[None, :], tq=512)
```

```python
import functools
import math

import jax
import jax.numpy as jnp
from jax import lax
from jax.experimental import pallas as pl
from jax.experimental.pallas import tpu as pltpu

F32 = jnp.float32
BF16 = jnp.bfloat16

GRID_W = 64
ROPE_THETA = 10000.0
NORM_EPS = 1e-6
SUBLN_EPS = 1e-5
LN_EPS = 1e-5
A_GROUPS = 8
A_CHUNK = 128
B_HEADS = 4
B_HEAD_DIM = 64
C_HEADS = 8
C_NOPE = 128
C_ROPE = 64
C_V = 128
C_Q_RANK = 256
C_KV_RANK = 128

LANES = 128
V7X_VMEM_LIMIT_CAP = 56 << 20
MOD_ROWS = 24

ROPE_SEG = 32
ROPE_HALF = 16
LOG2E = math.log2(math.e)
BOUND_SLACK = 1.0 + 2.0 ** -10
L_FLOOR = 2.0 ** -64
CTX_GROUP = 4

PROJ_ROWS = 1024
ATTN_ROWS = 512
MOD_COLS = 768
TEMP_HEADROOM = 8 << 20


def _vmem_limit(block_bytes, temp_bytes):
    return int(min(2 * block_bytes + temp_bytes, V7X_VMEM_LIMIT_CAP))


def _nbytes(shape, dtype):
    return math.prod(shape) * jnp.dtype(dtype).itemsize


def _silu(x):
    return x * (1.0 / (1.0 + jnp.exp(-x)))


def _gelu(x):
    return 0.5 * x * (1.0 + lax.erf(x * (2.0 ** -0.5)))


def _rms(x, w, eps):
    return x * lax.rsqrt(jnp.mean(x * x, axis=-1, keepdims=True) + eps) * w


def _dot(a, b):
    return jnp.dot(a, b, preferred_element_type=F32)


def _dot_nt(a, b):
    return lax.dot_general(a, b, (((1,), (1,)), ((), ())), preferred_element_type=F32)


def _rope128(x, cos, sin_lo, sin_hi):
    return (x * cos + pltpu.roll(x, LANES - ROPE_HALF, 1) * sin_lo
            + pltpu.roll(x, ROPE_HALF, 1) * sin_hi)


def _rope_tables(seq, dim):
    rows = seq // GRID_W
    row = jnp.repeat(jnp.arange(rows), GRID_W).astype(F32)
    col = jnp.tile(jnp.arange(GRID_W), rows).astype(F32)
    half = dim // 2
    inv = ROPE_THETA ** (-jnp.arange(0, half, 2, dtype=F32) / half)
    ang_r = row[:, None] * inv[None, :]
    ang_c = col[:, None] * inv[None, :]
    ang = jnp.concatenate([ang_r, ang_r, ang_c, ang_c], axis=-1)
    ang = jnp.tile(ang, (1, LANES // dim))
    first_half = (jnp.arange(LANES) % ROPE_SEG) < ROPE_HALF
    sin = jnp.sin(ang)
    return (jnp.cos(ang), jnp.where(first_half, -sin, 0.0), jnp.where(first_half, 0.0, sin))


def _attend_t(k, q, v_t):
    s = _dot_nt(k, q)
    m = jnp.max(s, axis=0, keepdims=True)
    p = jnp.exp2(s - m)
    l = jnp.sum(p, axis=0, keepdims=True)
    return _dot(v_t, p.astype(BF16)) * (1.0 / l)


def _attend_t_bounded(k, q, v_t, m):
    s = _dot_nt(k, q)
    p = jnp.exp2(s - m)
    l = jnp.sum(p, axis=0, keepdims=True)
    return _dot(v_t, p.astype(BF16)) * (1.0 / l), l


def _lane_pack(cols):
    rows = cols[0].shape[0]
    lane = lax.broadcasted_iota(jnp.int32, (rows, LANES), 1)
    out = jnp.zeros((rows, LANES), F32)
    for i, c in enumerate(cols):
        out = jnp.where(lane == i, c, out)
    return out


def _sumsq(x_bf16, mask=None):
    xf = x_bf16.astype(F32)
    sq = xf * xf
    if mask is not None:
        sq = jnp.where(mask, sq, 0.0)
    return jnp.sum(sq, axis=-1, keepdims=True)


def _score_bounds(q_sumsq, k_sumsq_ref):
    kmax = jnp.max(k_sumsq_ref[0], axis=0, keepdims=True)
    return (jnp.sqrt(q_sumsq * kmax) * BOUND_SLACK).T


def _mod_kernel(c_ref, w_ref, b_ref, o_ref):
    a = _silu(c_ref[...])
    w = w_ref[0]
    a_hi = a.astype(BF16)
    a_lo = (a - a_hi.astype(F32)).astype(BF16)
    w_hi = w.astype(BF16)
    w_lo = (w - w_hi.astype(F32)).astype(BF16)
    o_ref[0] = _dot(a_hi, w_hi) + (_dot(a_lo, w_hi) + _dot(a_hi, w_lo)) + b_ref[0]


def _modulation(cond, ada_w, ada_b):
    depth, d, n = ada_w.shape
    tn = MOD_COLS
    return pl.pallas_call(
        _mod_kernel,
        grid=(depth, n // tn),
        in_specs=[
            pl.BlockSpec((MOD_ROWS, d), lambda l, j: (0, 0)),
            pl.BlockSpec((1, d, tn), lambda l, j: (l, 0, j)),
            pl.BlockSpec((1, 1, tn), lambda l, j: (l, 0, j)),
        ],
        out_specs=pl.BlockSpec((1, MOD_ROWS, tn), lambda l, j: (l, 0, j)),
        out_shape=jax.ShapeDtypeStruct((depth, MOD_ROWS, n), F32),
        compiler_params=pltpu.CompilerParams(
            dimension_semantics=("arbitrary", "arbitrary"),
            vmem_limit_bytes=_vmem_limit(_nbytes((d, tn), F32), 2 * TEMP_HEADROOM)),
        name="modulation",
    )(cond, ada_w, ada_b.reshape(depth, 1, n))


def _store_rows(ref, val):
    n, r = ref.shape[0], ref.shape[1]
    for s in range(n):
        ref[s] = val[s * r:(s + 1) * r]


def _store_cols(ref, val):
    n, r = ref.shape[0], ref.shape[2]
    for s in range(n):
        ref[s] = val[:, s * r:(s + 1) * r]


def _transpose_cast_kernel(w_ref, o_ref):
    o_ref[...] = w_ref[0].T.astype(BF16)


def _transposed_columns(w, start, size):
    d = w.shape[1]
    return pl.pallas_call(
        _transpose_cast_kernel,
        grid=(1,),
        in_specs=[pl.BlockSpec((1, d, size), lambda i: (0, 0, start // size))],
        out_specs=pl.BlockSpec((size, d), lambda i: (0, 0)),
        out_shape=jax.ShapeDtypeStruct((size, d), BF16),
        name="transpose_cast",
    )(w)


def _modulated_input(h, mod_ref, row, normw_ref, d):
    shift = mod_ref[pl.ds(row, 1), 0:d]
    scale = mod_ref[pl.ds(row, 1), d:2 * d]
    xn = _rms(h, normw_ref[...], NORM_EPS)
    return (xn * (1.0 + scale) + shift).astype(BF16)


def _even_in_kernel(*refs, use_rope, n_alias, ctx_row, d, aw):
    refs = refs[:8] + refs[8 + n_alias:]
    if use_rope:
        (h_ref, mod_ref, normw_ref, win_ref, ws_ref, bias_ref, lnw_ref, lnb_ref,
         cos_ref, slo_ref, shi_ref, wvt_ref, a_ref, q_ref, k_ref, vt_ref, z_ref, kss_ref) = refs
    else:
        (h_ref, mod_ref, normw_ref, win_ref, ws_ref, bias_ref, lnw_ref, lnb_ref,
         wvt_ref, a_ref, q_ref, k_ref, vt_ref, z_ref, kss_ref) = refs
    row = pl.program_id(0) if ctx_row is None else ctx_row
    xl = _modulated_input(h_ref[0], mod_ref, row, normw_ref, d)
    tm = xl.shape[0]

    bq = _dot(xl, win_ref[:, 3 * aw:4 * aw]) * (B_HEAD_DIM ** -0.5 * LOG2E)
    bk = _dot(xl, win_ref[:, 4 * aw:5 * aw])
    if use_rope:
        cos, slo, shi = cos_ref[...], slo_ref[...], shi_ref[...]
        bq = jnp.concatenate([_rope128(bq[:, i * LANES:(i + 1) * LANES], cos, slo, shi)
                              for i in range(aw // LANES)], axis=1)
        bk = jnp.concatenate([_rope128(bk[:, i * LANES:(i + 1) * LANES], cos, slo, shi)
                              for i in range(aw // LANES)], axis=1)
    _store_rows(q_ref, bq.astype(BF16))
    kb = bk.astype(BF16)
    _store_rows(k_ref, kb)
    low = lax.broadcasted_iota(jnp.int32, (tm, LANES), 1) < B_HEAD_DIM
    _store_rows(kss_ref, _lane_pack(
        [_sumsq(kb[:, (i // 2) * LANES:(i // 2 + 1) * LANES], low if i % 2 == 0 else ~low)
         for i in range(2 * B_HEADS)]))
    _store_cols(vt_ref, _dot_nt(wvt_ref[...], xl).astype(BF16))
    _store_rows(z_ref, _silu(_dot(xl, win_ref[:, 6 * aw:7 * aw])).astype(BF16))

    uvz = _dot(xl, win_ref[:, 0:3 * aw])
    au, av, az = uvz[:, 0:aw], uvz[:, aw:2 * aw], uvz[:, 2 * aw:3 * aw]
    gv = _gelu(av)
    mu = jnp.mean(gv, axis=-1, keepdims=True)
    vc = gv - mu
    vn = (vc * lax.rsqrt(jnp.mean(vc * vc, axis=-1, keepdims=True) + LN_EPS) * lnw_ref[...]
          + lnb_ref[...]).astype(BF16)
    lane = lax.broadcasted_iota(jnp.int32, (A_CHUNK, LANES), 1)
    first_group = lane < (LANES // 2)
    zero = jnp.zeros((A_CHUNK, LANES), BF16)
    chunks = []
    for c in range(tm // A_CHUNK):
        cols = []
        for p in range(aw // LANES):
            vp = vn[c * A_CHUNK:(c + 1) * A_CHUNK, p * LANES:(p + 1) * LANES]
            rhs = jnp.concatenate([jnp.where(first_group, vp, zero), jnp.where(first_group, zero, vp)], axis=0)
            cols.append(_dot(ws_ref[p], rhs))
        chunks.append(jnp.concatenate(cols, axis=1) + bias_ref[...])
    mixed = jnp.concatenate(chunks, axis=0)
    _store_rows(a_ref, (_gelu(au) * mixed * _silu(az)).astype(BF16))


def _even_in(h, mod, norm_w, w_in, ws_pairs, bias, ln_w, ln_b, w_vt, rope, *, tm, ctx_row, total_rows,
             row_offset, alias, samples_per_tile=1):
    groups, rows, d = h.shape
    spt = samples_per_tile
    bsz = groups * spt
    sample_rows = tm // spt
    assert spt == 1 or rows == tm
    aw = ln_w.shape[-1]
    n_in = w_in.shape[1]
    use_rope = rope is not None
    off = row_offset // sample_rows
    const = lambda *_: (0, 0)
    in_specs = [
        pl.BlockSpec((1, tm, d), lambda b, j: (b, j, 0)),
        pl.BlockSpec(mod.shape, const),
        pl.BlockSpec((1, d), const),
        pl.BlockSpec((d, n_in), const),
        pl.BlockSpec(ws_pairs.shape, lambda *_: (0, 0, 0)),
        pl.BlockSpec(bias.shape, const),
        pl.BlockSpec((1, aw), const),
        pl.BlockSpec((1, aw), const),
    ]
    args = [h, mod, norm_w, w_in, ws_pairs, bias, ln_w, ln_b]
    aliases = {}
    if alias is not None:
        in_specs += [pl.BlockSpec(memory_space=pl.ANY)] * len(alias)
        aliases = {len(args) + i: i for i in range(len(alias))}
        args += list(alias)
    if use_rope:
        in_specs += [pl.BlockSpec((tm, LANES), lambda b, j: (j, 0))] * 3
        args += list(rope)
    in_specs += [pl.BlockSpec(w_vt.shape, const)]
    args += [w_vt]
    row_spec = pl.BlockSpec((spt, sample_rows, aw), lambda b, j: (b, j + off, 0))
    col_spec = pl.BlockSpec((spt, aw, sample_rows), lambda b, j: (b, 0, j + off))
    row_sds = jax.ShapeDtypeStruct((bsz, total_rows, aw), BF16)
    col_sds = jax.ShapeDtypeStruct((bsz, aw, total_rows), BF16)
    blocks = (_nbytes((tm, d), F32) + _nbytes((d, n_in + aw), BF16) + 5 * _nbytes((tm, aw), BF16)
              + _nbytes(mod.shape, F32) + 3 * _nbytes((tm, LANES), F32))
    return pl.pallas_call(
        functools.partial(_even_in_kernel, use_rope=use_rope, n_alias=len(aliases), ctx_row=ctx_row, d=d, aw=aw),
        grid=(groups, rows // tm),
        in_specs=in_specs,
        out_specs=[row_spec, row_spec, row_spec, col_spec, row_spec,
                   pl.BlockSpec((spt, sample_rows, LANES), lambda b, j: (b, j + off, 0))],
        out_shape=[row_sds, row_sds, row_sds, col_sds, row_sds,
                   jax.ShapeDtypeStruct((bsz, total_rows, LANES), F32)],
        input_output_aliases=aliases,
        compiler_params=pltpu.CompilerParams(
            dimension_semantics=("arbitrary", "arbitrary"),
            vmem_limit_bytes=_vmem_limit(blocks, 10 * _nbytes((tm, n_in), F32))),
        name="even_in_lat" if use_rope else "even_in_ctx",
    )(*args)


def _even_attn_kernel(h_ref, mod_ref, a_ref, q_ref, z_ref, k_ref, vt_ref, kss_ref, lq1_ref, lk1_ref, lq2_ref,
                      lk2_ref, subw_ref, wout_ref, o_ref, mix_ref, *, ctx_row, d, lam_init, tiles_per_sample):
    t = pl.program_id(0)
    slot = lax.rem(t, 2)
    tq = q_ref.shape[1]
    vdim = 2 * B_HEAD_DIM

    @pl.when(t == 0)
    def _():
        mix_ref[...] = jnp.zeros(mix_ref.shape, mix_ref.dtype)

    row = lax.div(jnp.maximum(t - 1, 0), tiles_per_sample) if ctx_row is None else ctx_row
    out = _dot(jnp.concatenate([a_ref[0], mix_ref[1 - slot]], axis=1), wout_ref[...])
    gate = mod_ref[pl.ds(row, 1), 2 * d:3 * d]
    o_ref[0] = h_ref[0] + gate * out

    lam = (jnp.exp(jnp.sum(lq1_ref[...] * lk1_ref[...], axis=-1, keepdims=True))
           - jnp.exp(jnp.sum(lq2_ref[...] * lk2_ref[...], axis=-1, keepdims=True)) + lam_init)

    lane = lax.broadcasted_iota(jnp.int32, (tq, vdim), 1)
    first_sub = lane < B_HEAD_DIM
    zero = jnp.zeros((tq, vdim), BF16)

    def run(exact):
        if not exact:
            qss = _lane_pack([_sumsq(q_ref[0, :, (i // 2) * vdim:(i // 2 + 1) * vdim],
                                     first_sub if i % 2 == 0 else ~first_sub) for i in range(2 * B_HEADS)])
            bounds = _score_bounds(qss, kss_ref)
        lmin = None
        for hd in range(B_HEADS):
            cs = slice(hd * vdim, (hd + 1) * vdim)
            qh = q_ref[0, :, cs]
            q2 = jnp.concatenate([jnp.where(first_sub, qh, zero), jnp.where(first_sub, zero, qh)], axis=0)
            if exact:
                o_t = _attend_t(k_ref[0, :, cs], q2, vt_ref[0, cs, :])
            else:
                m = jnp.concatenate([bounds[2 * hd:2 * hd + 1], bounds[2 * hd + 1:2 * hd + 2]], axis=1)
                o_t, l = _attend_t_bounded(k_ref[0, :, cs], q2, vt_ref[0, cs, :], m)
                lmin = l if lmin is None else jnp.minimum(lmin, l)
            o = (o_t[:, :tq] - lam * o_t[:, tq:]).T
            o = _rms(o, subw_ref[...], SUBLN_EPS) * (1.0 - lam_init)
            mix_ref[slot, :, cs] = (o * z_ref[0, :, cs].astype(F32)).astype(BF16)
        return lmin

    lmin = run(exact=False)

    @pl.when(jnp.logical_not(jnp.min(lmin) >= L_FLOOR))
    def _():
        run(exact=True)


def _even_attn(h, mod, a, q, z, k, v_t, kss, lparams, subln_w, w_out, *, tq, ctx_row, lam_init, q_offset,
               key_start, n_keys):
    bsz, rows, d = h.shape
    bw = q.shape[-1]
    nq = rows // tq
    n_tiles = bsz * nq
    const = lambda *_: (0, 0)
    qoff = q_offset // tq
    koff = key_start // n_keys
    cur = lambda t: jnp.minimum(t, n_tiles - 1)
    prev = lambda t: jnp.maximum(t - 1, 0)
    cur_tile = lambda w: pl.BlockSpec((1, tq, w), lambda t: (cur(t) // nq, cur(t) % nq + qoff, 0))
    in_specs = [pl.BlockSpec((1, tq, d), lambda t: (prev(t) // nq, prev(t) % nq, 0)),
                pl.BlockSpec(mod.shape, const),
                pl.BlockSpec((1, tq, bw), lambda t: (prev(t) // nq, prev(t) % nq + qoff, 0)),
                cur_tile(bw), cur_tile(bw),
                pl.BlockSpec((1, n_keys, bw), lambda t: (cur(t) // nq, koff, 0)),
                pl.BlockSpec((1, bw, n_keys), lambda t: (cur(t) // nq, 0, koff)),
                pl.BlockSpec((1, n_keys, LANES), lambda t: (cur(t) // nq, koff, 0))]
    in_specs += [pl.BlockSpec((1, B_HEAD_DIM), const)] * 4
    in_specs += [pl.BlockSpec(subln_w.shape, const), pl.BlockSpec(w_out.shape, const)]
    blocks = (2 * _nbytes((tq, d), F32) + 3 * _nbytes((tq, bw), BF16) + 2 * _nbytes((n_keys, bw), BF16)
              + _nbytes((n_keys, LANES), F32) + _nbytes(w_out.shape, BF16) + _nbytes(mod.shape, F32))
    temps = 3 * _nbytes((n_keys, 2 * tq), F32) + 2 * _nbytes((tq, bw), BF16)
    return pl.pallas_call(
        functools.partial(_even_attn_kernel, ctx_row=ctx_row, d=d, lam_init=lam_init, tiles_per_sample=nq),
        grid=(n_tiles + 1,),
        in_specs=in_specs,
        out_specs=pl.BlockSpec((1, tq, d), lambda t: (prev(t) // nq, prev(t) % nq, 0)),
        out_shape=jax.ShapeDtypeStruct((bsz, rows, d), F32),
        scratch_shapes=[pltpu.VMEM((2, tq, bw), BF16)],
        compiler_params=pltpu.CompilerParams(
            dimension_semantics=("arbitrary",),
            vmem_limit_bytes=_vmem_limit(blocks, temps + TEMP_HEADROOM)),
        name="even_attn_ctx" if ctx_row is not None else "even_attn_lat",
    )(h, mod, a, q, z, k, v_t, kss, *lparams, subln_w, w_out)


def _odd_in_kernel(*refs, is_ctx, ctx_row, d, scale):
    if is_ctx:
        (h_ref, mod_ref, normw_ref, win_ref, kvw_ref, wk_ref, wvt_ref, _, _, _, _,
         kn_ref, kr_ref, vt_ref, kss_ref) = refs
    else:
        (h_ref, mod_ref, normw_ref, win_ref, kvw_ref, wk_ref, wvt_ref, qw_ref, wq_ref,
         cos_ref, slo_ref, shi_ref, kn_ref, kr_ref, vt_ref, kss_ref, q_ref, z_ref) = refs
    row = pl.program_id(0) if ctx_row is None else ctx_row
    nk = C_HEADS * C_NOPE
    nr = C_HEADS * C_ROPE
    o_kv = C_Q_RANK
    o_kr = C_Q_RANK + C_KV_RANK
    o_z = o_kr + LANES
    xl = _modulated_input(h_ref[0], mod_ref, row, normw_ref, d)
    low_rank = _dot(xl, win_ref[:, 0:o_z])
    ckv = low_rank[:, o_kv:o_kr]
    kvn = _rms(ckv, kvw_ref[...], NORM_EPS).astype(BF16)
    knb = _dot(kvn, wk_ref[...]).astype(BF16)
    _store_rows(kn_ref, knb)
    _store_cols(vt_ref, _dot_nt(wvt_ref[...], kvn).astype(BF16))
    kr = low_rank[:, o_kr:o_z]
    if not is_ctx:
        cos, slo, shi = cos_ref[...], slo_ref[...], shi_ref[...]
        kr = _rope128(kr, cos, slo, shi)
    _store_rows(kr_ref, jnp.concatenate([kr, pltpu.roll(kr, LANES // 2, 1)], axis=1).astype(BF16))
    rope_ss = _sumsq(kr.astype(BF16))
    _store_rows(kss_ref, _lane_pack([_sumsq(knb[:, i * C_NOPE:(i + 1) * C_NOPE]) + rope_ss
                                     for i in range(C_HEADS)]))
    if is_ctx:
        return
    cq = low_rank[:, 0:o_kv]
    qn = _rms(cq, qw_ref[...], NORM_EPS).astype(BF16)
    q_ref[0, :, 0:nk] = (_dot(qn, wq_ref[:, 0:nk]) * scale).astype(BF16)
    qr = _dot(qn, wq_ref[:, nk:nk + nr]) * scale
    q_ref[0, :, nk:nk + nr] = jnp.concatenate(
        [_rope128(qr[:, i * LANES:(i + 1) * LANES], cos, slo, shi) for i in range(nr // LANES)],
        axis=1).astype(BF16)
    z_ref[0] = _silu(_dot(xl, win_ref[:, o_z:o_z + d])).astype(BF16)


def _odd_in(h, mod, norm_w, w_in, kv_norm_w, wk, wv_t, q_norm_w, wq, rope, *, tm, ctx_row, scale, total_rows,
            row_offset, alias, samples_per_tile=1):
    groups, rows, d = h.shape
    spt = samples_per_tile
    bsz = groups * spt
    sample_rows = tm // spt
    assert spt == 1 or rows == tm
    is_ctx = rope is None
    const = lambda *_: (0, 0)
    off = row_offset // sample_rows
    nk = C_HEADS * C_NOPE
    nq = nk + C_HEADS * C_ROPE
    in_specs = [
        pl.BlockSpec((1, tm, d), lambda b, j: (b, j, 0)),
        pl.BlockSpec(mod.shape, const),
        pl.BlockSpec((1, d), const),
        pl.BlockSpec(w_in.shape, const),
        pl.BlockSpec(kv_norm_w.shape, const),
        pl.BlockSpec(wk.shape, const),
        pl.BlockSpec(wv_t.shape, const),
    ]
    args = [h, mod, norm_w, w_in, kv_norm_w, wk, wv_t]
    row_spec = lambda w: pl.BlockSpec((spt, sample_rows, w), lambda b, j: (b, j + off, 0))
    row_sds = lambda w, r: jax.ShapeDtypeStruct((bsz, r, w), BF16)
    out_specs = [row_spec(nk), row_spec(2 * LANES),
                 pl.BlockSpec((spt, nk, sample_rows), lambda b, j: (b, 0, j + off)), row_spec(LANES)]
    out_shape = [row_sds(nk, total_rows), row_sds(2 * LANES, total_rows),
                 jax.ShapeDtypeStruct((bsz, nk, total_rows), BF16),
                 jax.ShapeDtypeStruct((bsz, total_rows, LANES), F32)]
    aliases = {}
    if is_ctx:
        in_specs += [pl.BlockSpec(memory_space=pl.ANY)] * len(alias)
        aliases = {len(args) + i: i for i in range(len(alias))}
        args += list(alias)
    else:
        in_specs += [pl.BlockSpec(q_norm_w.shape, const), pl.BlockSpec(wq.shape, const)]
        in_specs += [pl.BlockSpec((tm, LANES), lambda b, j: (j, 0))] * 3
        args += [q_norm_w, wq] + list(rope)
        tile = lambda w: pl.BlockSpec((1, tm, w), lambda b, j: (b, j, 0))
        out_specs += [tile(nq), tile(d)]
        out_shape += [row_sds(nq, rows), row_sds(d, rows)]
    blocks = (_nbytes((tm, d), F32) + _nbytes(w_in.shape, BF16) + 2 * _nbytes(wk.shape, BF16)
              + _nbytes((C_Q_RANK, nq), BF16) + _nbytes((tm, 2 * nk + 2 * LANES + nq + d), BF16)
              + _nbytes(mod.shape, F32) + 3 * _nbytes((tm, LANES), F32))
    return pl.pallas_call(
        functools.partial(_odd_in_kernel, is_ctx=is_ctx, ctx_row=ctx_row, d=d, scale=scale),
        grid=(groups, rows // tm),
        in_specs=in_specs,
        out_specs=out_specs,
        out_shape=out_shape,
        input_output_aliases=aliases,
        compiler_params=pltpu.CompilerParams(
            dimension_semantics=("arbitrary", "arbitrary"),
            vmem_limit_bytes=_vmem_limit(blocks, 12 * _nbytes((tm, 2 * nk), F32))),
        name="odd_in_ctx" if is_ctx else "odd_in_lat",
    )(*args)


def _odd_attn_kernel(h_ref, mod_ref, q_ref, z_ref, kn_ref, kr_ref, vt_ref, kss_ref, wout_ref, fw_ref,
                     o_ref, mix_ref, *, d, tiles_per_sample):
    t = pl.program_id(0)
    slot = lax.rem(t, 2)
    nk = C_HEADS * C_NOPE

    @pl.when(t == 0)
    def _():
        mix_ref[...] = jnp.zeros(mix_ref.shape, mix_ref.dtype)

    prev_sample = lax.div(jnp.maximum(t - 1, 0), tiles_per_sample)
    out = _dot(mix_ref[1 - slot], wout_ref[...])
    gate = mod_ref[pl.ds(prev_sample, 1), 2 * d:3 * d]
    h2 = h_ref[0] + gate * out
    o_ref[0] = _rms(h2, fw_ref[...], NORM_EPS)

    def q_head(hd):
        pair = nk + (hd // 2) * LANES
        return q_ref[0, :, hd * C_NOPE:(hd + 1) * C_NOPE], q_ref[0, :, pair:pair + LANES]

    def run(exact):
        if not exact:
            low = lax.broadcasted_iota(jnp.int32, (q_ref.shape[1], LANES), 1) < C_ROPE
            qss = _lane_pack([_sumsq(q_head(hd)[0]) + _sumsq(q_head(hd)[1], low if hd % 2 == 0 else ~low)
                              for hd in range(C_HEADS)])
            bounds = _score_bounds(qss, kss_ref)
        lmin = None
        for hd in range(C_HEADS):
            cs = slice(hd * C_NOPE, (hd + 1) * C_NOPE)
            qh = jnp.concatenate(q_head(hd), axis=1)
            rs = slice((hd % 2) * LANES, (hd % 2 + 1) * LANES)
            kh = jnp.concatenate([kn_ref[0, :, cs], kr_ref[0, :, rs]], axis=1)
            if exact:
                o_t = _attend_t(kh, qh, vt_ref[0, cs, :])
            else:
                o_t, l = _attend_t_bounded(kh, qh, vt_ref[0, cs, :], bounds[hd:hd + 1])
                lmin = l if lmin is None else jnp.minimum(lmin, l)
            mix_ref[slot, :, cs] = (o_t.T * z_ref[0, :, cs].astype(F32)).astype(BF16)
        return lmin

    lmin = run(exact=False)

    @pl.when(jnp.logical_not(jnp.min(lmin) >= L_FLOOR))
    def _():
        run(exact=True)


def _odd_attn(h, mod, q, z, kn, kr, v_t, kss, w_out, final_w, *, tq):
    bsz, rows, d = h.shape
    nq = rows // tq
    n_tiles = bsz * nq
    const = lambda *_: (0, 0)
    cur = lambda t: jnp.minimum(t, n_tiles - 1)
    prev = lambda t: jnp.maximum(t - 1, 0)
    cur_tile = lambda w: pl.BlockSpec((1, tq, w), lambda t: (cur(t) // nq, cur(t) % nq, 0))
    prev_tile = lambda w: pl.BlockSpec((1, tq, w), lambda t: (prev(t) // nq, prev(t) % nq, 0))
    full = lambda arr: pl.BlockSpec((1,) + arr.shape[1:], lambda t: (cur(t) // nq, 0, 0))
    kv_arrays = [kn, kr, v_t, kss]
    n_keys = kn.shape[1]
    in_specs = ([prev_tile(d), pl.BlockSpec(mod.shape, const), cur_tile(q.shape[-1]), cur_tile(d)]
                + [full(a) for a in kv_arrays]
                + [pl.BlockSpec(w_out.shape, const), pl.BlockSpec((1, d), const)])
    blocks = (2 * _nbytes((tq, d), F32) + _nbytes((tq, q.shape[-1] + d), BF16)
              + sum(_nbytes(a.shape[1:], a.dtype) for a in kv_arrays)
              + _nbytes(w_out.shape, BF16) + _nbytes(mod.shape, F32))
    temps = 4 * _nbytes((n_keys, tq), F32) + 2 * _nbytes((tq, d), BF16)
    return pl.pallas_call(
        functools.partial(_odd_attn_kernel, d=d, tiles_per_sample=nq),
        grid=(n_tiles + 1,),
        in_specs=in_specs,
        out_specs=prev_tile(d),
        out_shape=jax.ShapeDtypeStruct((bsz, rows, d), F32),
        scratch_shapes=[pltpu.VMEM((2, tq, d), BF16)],
        compiler_params=pltpu.CompilerParams(
            dimension_semantics=("arbitrary",),
            vmem_limit_bytes=_vmem_limit(blocks, temps + TEMP_HEADROOM)),
        name="odd_attn",
    )(h, mod, q, z, *kv_arrays, w_out, final_w)


def kernel(x, c, ctx, c_ctx, norm_w, ada_w, ada_b, even_w_in, a_ws, a_bs, a_ln_w, a_ln_b, b_lq1, b_lk1,
           b_lq2, b_lk2, b_subln_w, even_w_out, odd_w_in, c_q_norm_w, c_wq_b, c_kv_norm_w, c_wkv_b,
           odd_w_out, final_w):
    bsz, seq, d = x.shape
    clen = ctx.shape[1]
    ntok = seq + clen
    assert bsz + 1 <= MOD_ROWS and seq % GRID_W == 0 and B_HEAD_DIM == C_ROPE and seq % clen == 0
    assert bsz % CTX_GROUP == 0
    ctx_row = bsz

    cond = jnp.concatenate([c, c_ctx[None, :], jnp.zeros((MOD_ROWS - bsz - 1, d), F32)], axis=0)
    mod = _modulation(cond, ada_w, ada_b)
    rope = _rope_tables(seq, C_ROPE)

    aw = a_ln_w.shape[-1]
    w_in0 = even_w_in[0].astype(BF16)
    w_vt0 = _transposed_columns(even_w_in, 5 * aw, aw)
    ws = a_ws[0].astype(BF16)
    ws_pairs = jnp.concatenate([ws[0::2], ws[1::2]], axis=2)
    bias = jnp.repeat(a_bs[0].T, aw // A_GROUPS, axis=1)
    ln_w, ln_b = a_ln_w[0][None, :], a_ln_b[0][None, :]
    lparams = [p[0][None, :] for p in (b_lq1, b_lk1, b_lq2, b_lk2)]
    subln_w = b_subln_w[0][None, :]
    w_out0 = even_w_out[0].astype(BF16)
    lam_init = 0.8 - 0.6 * math.exp(-0.3 * 0)
    nw0 = norm_w[0][None, :]

    even_lat = _even_in(x, mod[0], nw0, w_in0, ws_pairs, bias, ln_w, ln_b, w_vt0, rope,
                        tm=PROJ_ROWS, ctx_row=None, total_rows=ntok, row_offset=0, alias=None)
    grouped = lambda t: t.reshape(bsz // CTX_GROUP, CTX_GROUP * clen, d)
    a_t, q_t, k_t, vt_t, z_t, kss_t = _even_in(grouped(ctx), mod[0], nw0, w_in0, ws_pairs, bias, ln_w, ln_b, w_vt0,
                                               None, tm=CTX_GROUP * clen, ctx_row=ctx_row, total_rows=ntok,
                                               row_offset=seq, alias=even_lat, samples_per_tile=CTX_GROUP)
    h_lat = _even_attn(x, mod[0], a_t, q_t, z_t, k_t, vt_t, kss_t, lparams, subln_w, w_out0,
                       tq=ATTN_ROWS, ctx_row=None, lam_init=lam_init, q_offset=0, key_start=0, n_keys=ntok)
    h_ctx = _even_attn(ctx, mod[0], a_t, q_t, z_t, k_t, vt_t, kss_t, lparams, subln_w, w_out0,
                       tq=clen, ctx_row=ctx_row, lam_init=lam_init, q_offset=seq, key_start=seq, n_keys=clen)

    w1 = odd_w_in[0]
    o_kr = C_Q_RANK + C_KV_RANK
    w_in1 = jnp.concatenate([w1[:, :o_kr + C_ROPE], jnp.zeros((d, LANES - C_ROPE), F32),
                             w1[:, o_kr + C_ROPE:]], axis=1).astype(BF16)
    wq = c_wq_b[0].reshape(C_Q_RANK, C_HEADS, C_NOPE + C_ROPE)
    wq = jnp.concatenate([wq[:, :, :C_NOPE].reshape(C_Q_RANK, -1),
                          wq[:, :, C_NOPE:].reshape(C_Q_RANK, -1)], axis=1).astype(BF16)
    wkv = c_wkv_b[0].reshape(C_KV_RANK, C_HEADS, C_NOPE + C_V)
    wk = wkv[:, :, :C_NOPE].reshape(C_KV_RANK, -1).astype(BF16)
    wv_t = wkv[:, :, C_NOPE:].reshape(C_KV_RANK, -1).T.astype(BF16)
    qnw, kvnw = c_q_norm_w[0][None, :], c_kv_norm_w[0][None, :]
    w_out1 = odd_w_out[0].astype(BF16)
    nw1 = norm_w[1][None, :]
    scale = (C_NOPE + C_ROPE) ** -0.5 * LOG2E

    kn_l, kr_l, vt_l, kss_l, qq, zz = _odd_in(h_lat, mod[1], nw1, w_in1, kvnw, wk, wv_t, qnw, wq, rope,
                                              tm=PROJ_ROWS, ctx_row=None, scale=scale, total_rows=ntok, row_offset=0,
                                              alias=None)
    kn, kr, vt, kss = _odd_in(grouped(h_ctx), mod[1], nw1, w_in1, kvnw, wk, wv_t, None, None, None,
                              tm=CTX_GROUP * clen, ctx_row=ctx_row, scale=scale, total_rows=ntok, row_offset=seq,
                              alias=(kn_l, kr_l, vt_l, kss_l), samples_per_tile=CTX_GROUP)
    return _odd_attn(h_lat, mod[1], qq, zz, kn, kr, vt, kss, w_out1, final_w[None, :], tq=ATTN_ROWS)
```

```python
import functools
import math

import jax
import jax.numpy as jnp
from jax import lax
from jax.experimental import pallas as pl
from jax.experimental.pallas import tpu as pltpu

F32 = jnp.float32
BF16 = jnp.bfloat16

GRID_W = 64
ROPE_THETA = 10000.0
NORM_EPS = 1e-6
SUBLN_EPS = 1e-5
LN_EPS = 1e-5
A_GROUPS = 8
A_CHUNK = 128
B_HEADS = 4
B_HEAD_DIM = 64
C_HEADS = 8
C_NOPE = 128
C_ROPE = 64
C_V = 128
C_Q_RANK = 256
C_KV_RANK = 128

LANES = 128
V7X_VMEM_LIMIT_CAP = 56 << 20
MOD_ROWS = 24

ROPE_SEG = 32
ROPE_HALF = 16
LOG2E = math.log2(math.e)
BOUND_SLACK = 1.0 + 2.0 ** -10
L_FLOOR = 2.0 ** -64
CTX_GROUP = 4

PROJ_ROWS = 1024
ATTN_ROWS = 512
MOD_COLS = 768
TEMP_HEADROOM = 8 << 20


def _vmem_limit(block_bytes, temp_bytes):
    return int(min(2 * block_bytes + temp_bytes, V7X_VMEM_LIMIT_CAP))


def _nbytes(shape, dtype):
    return math.prod(shape) * jnp.dtype(dtype).itemsize


def _silu(x):
    return x * (1.0 / (1.0 + jnp.exp(-x)))


def _gelu(x):
    return 0.5 * x * (1.0 + lax.erf(x * (2.0 ** -0.5)))


def _rms(x, w, eps):
    return x * lax.rsqrt(jnp.mean(x * x, axis=-1, keepdims=True) + eps) * w


def _dot(a, b):
    return jnp.dot(a, b, preferred_element_type=F32)


def _dot_nt(a, b):
    return lax.dot_general(a, b, (((1,), (1,)), ((), ())), preferred_element_type=F32)


def _rope128(x, cos, sin_lo, sin_hi):
    return (x * cos + pltpu.roll(x, LANES - ROPE_HALF, 1) * sin_lo
            + pltpu.roll(x, ROPE_HALF, 1) * sin_hi)


def _rope_tables(seq, dim):
    rows = seq // GRID_W
    row = jnp.repeat(jnp.arange(rows), GRID_W).astype(F32)
    col = jnp.tile(jnp.arange(GRID_W), rows).astype(F32)
    half = dim // 2
    inv = ROPE_THETA ** (-jnp.arange(0, half, 2, dtype=F32) / half)
    ang_r = row[:, None] * inv[None, :]
    ang_c = col[:, None] * inv[None, :]
    ang = jnp.concatenate([ang_r, ang_r, ang_c, ang_c], axis=-1)
    ang = jnp.tile(ang, (1, LANES // dim))
    first_half = (jnp.arange(LANES) % ROPE_SEG) < ROPE_HALF
    sin = jnp.sin(ang)
    return (jnp.cos(ang), jnp.where(first_half, -sin, 0.0), jnp.where(first_half, 0.0, sin))


def _attend_t(k, q, v_t):
    s = _dot_nt(k, q)
    m = jnp.max(s, axis=0, keepdims=True)
    p = jnp.exp2(s - m)
    l = jnp.sum(p, axis=0, keepdims=True)
    return _dot(v_t, p.astype(BF16)) * (1.0 / l)


def _attend_t_bounded(k, q, v_t, m):
    s = _dot_nt(k, q)
    p = jnp.exp2(s - m)
    l = jnp.sum(p, axis=0, keepdims=True)
    return _dot(v_t, p.astype(BF16)) * (1.0 / l), l


def _lane_pack(cols):
    rows = cols[0].shape[0]
    lane = lax.broadcasted_iota(jnp.int32, (rows, LANES), 1)
    out = jnp.zeros((rows, LANES), F32)
    for i, c in enumerate(cols):
        out = jnp.where(lane == i, c, out)
    return out


def _sumsq(x_bf16, mask=None):
    xf = x_bf16.astype(F32)
    sq = xf * xf
    if mask is not None:
        sq = jnp.where(mask, sq, 0.0)
    return jnp.sum(sq, axis=-1, keepdims=True)


def _score_bounds(q_sumsq, k_sumsq_ref):
    kmax = jnp.max(k_sumsq_ref[0], axis=0, keepdims=True)
    return (jnp.sqrt(q_sumsq * kmax) * BOUND_SLACK).T


def _mod_kernel(c_ref, w_ref, b_ref, o_ref):
    a = _silu(c_ref[...])
    w = w_ref[0]
    a_hi = a.astype(BF16)
    a_lo = (a - a_hi.astype(F32)).astype(BF16)
    w_hi = w.astype(BF16)
    w_lo = (w - w_hi.astype(F32)).astype(BF16)
    o_ref[0] = _dot(a_hi, w_hi) + (_dot(a_lo, w_hi) + _dot(a_hi, w_lo)) + b_ref[0]


def _modulation(cond, ada_w, ada_b):
    depth, d, n = ada_w.shape
    tn = MOD_COLS
    return pl.pallas_call(
        _mod_kernel,
        grid=(depth, n // tn),
        in_specs=[
            pl.BlockSpec((MOD_ROWS, d), lambda l, j: (0, 0)),
            pl.BlockSpec((1, d, tn), lambda l, j: (l, 0, j)),
            pl.BlockSpec((1, 1, tn), lambda l, j: (l, 0, j)),
        ],
        out_specs=pl.BlockSpec((1, MOD_ROWS, tn), lambda l, j: (l, 0, j)),
        out_shape=jax.ShapeDtypeStruct((depth, MOD_ROWS, n), F32),
        compiler_params=pltpu.CompilerParams(
            dimension_semantics=("arbitrary", "arbitrary"),
            vmem_limit_bytes=_vmem_limit(_nbytes((d, tn), F32), 2 * TEMP_HEADROOM)),
        name="modulation",
    )(cond, ada_w, ada_b.reshape(depth, 1, n))


def _store_rows(ref, val):
    n, r = ref.shape[0], ref.shape[1]
    for s in range(n):
        ref[s] = val[s * r:(s + 1) * r]


def _store_cols(ref, val):
    n, r = ref.shape[0], ref.shape[2]
    for s in range(n):
        ref[s] = val[:, s * r:(s + 1) * r]


def _transpose_cast_kernel(w_ref, o_ref):
    o_ref[...] = w_ref[0].T.astype(BF16)


def _transposed_columns(w, start, size):
    d = w.shape[1]
    return pl.pallas_call(
        _transpose_cast_kernel,
        grid=(1,),
        in_specs=[pl.BlockSpec((1, d, size), lambda i: (0, 0, start // size))],
        out_specs=pl.BlockSpec((size, d), lambda i: (0, 0)),
        out_shape=jax.ShapeDtypeStruct((size, d), BF16),
        name="transpose_cast",
    )(w)


def _modulated_input(h, mod_ref, row, normw_ref, d):
    shift = mod_ref[pl.ds(row, 1), 0:d]
    scale = mod_ref[pl.ds(row, 1), d:2 * d]
    xn = _rms(h, normw_ref[...], NORM_EPS)
    return (xn * (1.0 + scale) + shift).astype(BF16)


def _even_in_kernel(*refs, use_rope, n_alias, ctx_row, d, aw):
    refs = refs[:8] + refs[8 + n_alias:]
    if use_rope:
        (h_ref, mod_ref, normw_ref, win_ref, ws_ref, bias_ref, lnw_ref, lnb_ref,
         cos_ref, slo_ref, shi_ref, wvt_ref, a_ref, q_ref, k_ref, vt_ref, z_ref, kss_ref) = refs
    else:
        (h_ref, mod_ref, normw_ref, win_ref, ws_ref, bias_ref, lnw_ref, lnb_ref,
         wvt_ref, a_ref, q_ref, k_ref, vt_ref, z_ref, kss_ref) = refs
    row = pl.program_id(0) if ctx_row is None else ctx_row
    xl = _modulated_input(h_ref[0], mod_ref, row, normw_ref, d)
    tm = xl.shape[0]

    bq = _dot(xl, win_ref[:, 3 * aw:4 * aw]) * (B_HEAD_DIM ** -0.5 * LOG2E)
    bk = _dot(xl, win_ref[:, 4 * aw:5 * aw])
    if use_rope:
        cos, slo, shi = cos_ref[...], slo_ref[...], shi_ref[...]
        bq = jnp.concatenate([_rope128(bq[:, i * LANES:(i + 1) * LANES], cos, slo, shi)
                              for i in range(aw // LANES)], axis=1)
        bk = jnp.concatenate([_rope128(bk[:, i * LANES:(i + 1) * LANES], cos, slo, shi)
                              for i in range(aw // LANES)], axis=1)
    _store_rows(q_ref, bq.astype(BF16))
    kb = bk.astype(BF16)
    _store_rows(k_ref, kb)
    low = lax.broadcasted_iota(jnp.int32, (tm, LANES), 1) < B_HEAD_DIM
    _store_rows(kss_ref, _lane_pack(
        [_sumsq(kb[:, (i // 2) * LANES:(i // 2 + 1) * LANES], low if i % 2 == 0 else ~low)
         for i in range(2 * B_HEADS)]))
    _store_cols(vt_ref, _dot_nt(wvt_ref[...], xl).astype(BF16))
    _store_rows(z_ref, _silu(_dot(xl, win_ref[:, 6 * aw:7 * aw])).astype(BF16))

    uvz = _dot(xl, win_ref[:, 0:3 * aw])
    au, av, az = uvz[:, 0:aw], uvz[:, aw:2 * aw], uvz[:, 2 * aw:3 * aw]
    gv = _gelu(av)
    mu = jnp.mean(gv, axis=-1, keepdims=True)
    vc = gv - mu
    vn = (vc * lax.rsqrt(jnp.mean(vc * vc, axis=-1, keepdims=True) + LN_EPS) * lnw_ref[...]
          + lnb_ref[...]).astype(BF16)
    lane = lax.broadcasted_iota(jnp.int32, (A_CHUNK, LANES), 1)
    first_group = lane < (LANES // 2)
    zero = jnp.zeros((A_CHUNK, LANES), BF16)
    chunks = []
    for c in range(tm // A_CHUNK):
        cols = []
        for p in range(aw // LANES):
            vp = vn[c * A_CHUNK:(c + 1) * A_CHUNK, p * LANES:(p + 1) * LANES]
            rhs = jnp.concatenate([jnp.where(first_group, vp, zero), jnp.where(first_group, zero, vp)], axis=0)
            cols.append(_dot(ws_ref[p], rhs))
        chunks.append(jnp.concatenate(cols, axis=1) + bias_ref[...])
    mixed = jnp.concatenate(chunks, axis=0)
    _store_rows(a_ref, (_gelu(au) * mixed * _silu(az)).astype(BF16))


def _even_in(h, mod, norm_w, w_in, ws_pairs, bias, ln_w, ln_b, w_vt, rope, *, tm, ctx_row, total_rows,
             row_offset, alias, samples_per_tile=1):
    groups, rows, d = h.shape
    spt = samples_per_tile
    bsz = groups * spt
    sample_rows = tm // spt
    assert spt == 1 or rows == tm
    aw = ln_w.shape[-1]
    n_in = w_in.shape[1]
    use_rope = rope is not None
    off = row_offset // sample_rows
    const = lambda *_: (0, 0)
    in_specs = [
        pl.BlockSpec((1, tm, d), lambda b, j: (b, j, 0)),
        pl.BlockSpec(mod.shape, const),
        pl.BlockSpec((1, d), const),
        pl.BlockSpec((d, n_in), const, pipeline_mode=pl.Buffered(1)),
        pl.BlockSpec(ws_pairs.shape, lambda *_: (0, 0, 0)),
        pl.BlockSpec(bias.shape, const),
        pl.BlockSpec((1, aw), const),
        pl.BlockSpec((1, aw), const),
    ]
    args = [h, mod, norm_w, w_in, ws_pairs, bias, ln_w, ln_b]
    aliases = {}
    if alias is not None:
        in_specs += [pl.BlockSpec(memory_space=pl.ANY)] * len(alias)
        aliases = {len(args) + i: i for i in range(len(alias))}
        args += list(alias)
    if use_rope:
        in_specs += [pl.BlockSpec((tm, LANES), lambda b, j: (j, 0))] * 3
        args += list(rope)
    in_specs += [pl.BlockSpec(w_vt.shape, const, pipeline_mode=pl.Buffered(1))]
    args += [w_vt]
    row_spec = pl.BlockSpec((spt, sample_rows, aw), lambda b, j: (b, j + off, 0))
    col_spec = pl.BlockSpec((spt, aw, sample_rows), lambda b, j: (b, 0, j + off))
    row_sds = jax.ShapeDtypeStruct((bsz, total_rows, aw), BF16)
    col_sds = jax.ShapeDtypeStruct((bsz, aw, total_rows), BF16)
    blocks = (_nbytes((tm, d), F32) + _nbytes((d, n_in + aw), BF16) + 5 * _nbytes((tm, aw), BF16)
              + _nbytes(mod.shape, F32) + 3 * _nbytes((tm, LANES), F32))
    return pl.pallas_call(
        functools.partial(_even_in_kernel, use_rope=use_rope, n_alias=len(aliases), ctx_row=ctx_row, d=d, aw=aw),
        grid=(groups, rows // tm),
        in_specs=in_specs,
        out_specs=[row_spec, row_spec, row_spec, col_spec, row_spec,
                   pl.BlockSpec((spt, sample_rows, LANES), lambda b, j: (b, j + off, 0))],
        out_shape=[row_sds, row_sds, row_sds, col_sds, row_sds,
                   jax.ShapeDtypeStruct((bsz, total_rows, LANES), F32)],
        input_output_aliases=aliases,
        compiler_params=pltpu.CompilerParams(
            dimension_semantics=("arbitrary", "arbitrary"),
            vmem_limit_bytes=_vmem_limit(blocks, 10 * _nbytes((tm, n_in), F32))),
        name="even_in_lat" if use_rope else "even_in_ctx",
    )(*args)


def _even_attn_kernel(h_ref, mod_ref, a_ref, q_ref, z_ref, k_ref, vt_ref, kss_ref, lq1_ref, lk1_ref, lq2_ref,
                      lk2_ref, subw_ref, wout_ref, o_ref, mix_ref, *, ctx_row, d, lam_init, tiles_per_sample):
    t = pl.program_id(0)
    slot = lax.rem(t, 2)
    tq = q_ref.shape[1]
    vdim = 2 * B_HEAD_DIM

    @pl.when(t == 0)
    def _():
        mix_ref[...] = jnp.zeros(mix_ref.shape, mix_ref.dtype)

    row = lax.div(jnp.maximum(t - 1, 0), tiles_per_sample) if ctx_row is None else ctx_row
    out = _dot(jnp.concatenate([a_ref[0], mix_ref[1 - slot]], axis=1), wout_ref[...])
    gate = mod_ref[pl.ds(row, 1), 2 * d:3 * d]
    o_ref[0] = h_ref[0] + gate * out

    lam = (jnp.exp(jnp.sum(lq1_ref[...] * lk1_ref[...], axis=-1, keepdims=True))
           - jnp.exp(jnp.sum(lq2_ref[...] * lk2_ref[...], axis=-1, keepdims=True)) + lam_init)

    lane = lax.broadcasted_iota(jnp.int32, (tq, vdim), 1)
    first_sub = lane < B_HEAD_DIM
    zero = jnp.zeros((tq, vdim), BF16)

    def run(exact):
        if not exact:
            qss = _lane_pack([_sumsq(q_ref[0, :, (i // 2) * vdim:(i // 2 + 1) * vdim],
                                     first_sub if i % 2 == 0 else ~first_sub) for i in range(2 * B_HEADS)])
            bounds = _score_bounds(qss, kss_ref)
        lmin = None
        for hd in range(B_HEADS):
            cs = slice(hd * vdim, (hd + 1) * vdim)
            qh = q_ref[0, :, cs]
            q2 = jnp.concatenate([jnp.where(first_sub, qh, zero), jnp.where(first_sub, zero, qh)], axis=0)
            if exact:
                o_t = _attend_t(k_ref[0, :, cs], q2, vt_ref[0, cs, :])
            else:
                m = jnp.concatenate([bounds[2 * hd:2 * hd + 1], bounds[2 * hd + 1:2 * hd + 2]], axis=1)
                o_t, l = _attend_t_bounded(k_ref[0, :, cs], q2, vt_ref[0, cs, :], m)
                lmin = l if lmin is None else jnp.minimum(lmin, l)
            o = (o_t[:, :tq] - lam * o_t[:, tq:]).T
            o = _rms(o, subw_ref[...], SUBLN_EPS) * (1.0 - lam_init)
            mix_ref[slot, :, cs] = (o * z_ref[0, :, cs].astype(F32)).astype(BF16)
        return lmin

    lmin = run(exact=False)

    @pl.when(jnp.logical_not(jnp.min(lmin) >= L_FLOOR))
    def _():
        run(exact=True)


def _even_attn(h, mod, a, q, z, k, v_t, kss, lparams, subln_w, w_out, *, tq, ctx_row, lam_init, q_offset,
               key_start, n_keys):
    bsz, rows, d = h.shape
    bw = q.shape[-1]
    nq = rows // tq
    n_tiles = bsz * nq
    const = lambda *_: (0, 0)
    qoff = q_offset // tq
    koff = key_start // n_keys
    cur = lambda t: jnp.minimum(t, n_tiles - 1)
    prev = lambda t: jnp.maximum(t - 1, 0)
    cur_tile = lambda w: pl.BlockSpec((1, tq, w), lambda t: (cur(t) // nq, cur(t) % nq + qoff, 0))
    in_specs = [pl.BlockSpec((1, tq, d), lambda t: (prev(t) // nq, prev(t) % nq, 0)),
                pl.BlockSpec(mod.shape, const),
                pl.BlockSpec((1, tq, bw), lambda t: (prev(t) // nq, prev(t) % nq + qoff, 0)),
                cur_tile(bw), cur_tile(bw),
                pl.BlockSpec((1, n_keys, bw), lambda t: (cur(t) // nq, koff, 0)),
                pl.BlockSpec((1, bw, n_keys), lambda t: (cur(t) // nq, 0, koff)),
                pl.BlockSpec((1, n_keys, LANES), lambda t: (cur(t) // nq, koff, 0))]
    in_specs += [pl.BlockSpec((1, B_HEAD_DIM), const)] * 4
    in_specs += [pl.BlockSpec(subln_w.shape, const),
                 pl.BlockSpec(w_out.shape, const, pipeline_mode=pl.Buffered(1))]
    blocks = (2 * _nbytes((tq, d), F32) + 3 * _nbytes((tq, bw), BF16) + 2 * _nbytes((n_keys, bw), BF16)
              + _nbytes((n_keys, LANES), F32) + _nbytes(w_out.shape, BF16) + _nbytes(mod.shape, F32))
    temps = 3 * _nbytes((n_keys, 2 * tq), F32) + 2 * _nbytes((tq, bw), BF16)
    return pl.pallas_call(
        functools.partial(_even_attn_kernel, ctx_row=ctx_row, d=d, lam_init=lam_init, tiles_per_sample=nq),
        grid=(n_tiles + 1,),
        in_specs=in_specs,
        out_specs=pl.BlockSpec((1, tq, d), lambda t: (prev(t) // nq, prev(t) % nq, 0)),
        out_shape=jax.ShapeDtypeStruct((bsz, rows, d), F32),
        scratch_shapes=[pltpu.VMEM((2, tq, bw), BF16)],
        compiler_params=pltpu.CompilerParams(
            dimension_semantics=("arbitrary",),
            vmem_limit_bytes=_vmem_limit(blocks, temps + TEMP_HEADROOM)),
        name="even_attn_ctx" if ctx_row is not None else "even_attn_lat",
    )(h, mod, a, q, z, k, v_t, kss, *lparams, subln_w, w_out)


def _odd_in_kernel(*refs, is_ctx, ctx_row, d, scale):
    if is_ctx:
        (h_ref, mod_ref, normw_ref, win_ref, kvw_ref, wk_ref, wvt_ref, _, _, _, _,
         kn_ref, kr_ref, vt_ref, kss_ref) = refs
    else:
        (h_ref, mod_ref, normw_ref, win_ref, kvw_ref, wk_ref, wvt_ref, qw_ref, wq_ref,
         cos_ref, slo_ref, shi_ref, kn_ref, kr_ref, vt_ref, kss_ref, q_ref, z_ref) = refs
    row = pl.program_id(0) if ctx_row is None else ctx_row
    nk = C_HEADS * C_NOPE
    nr = C_HEADS * C_ROPE
    o_kv = C_Q_RANK
    o_kr = C_Q_RANK + C_KV_RANK
    o_z = o_kr + LANES
    xl = _modulated_input(h_ref[0], mod_ref, row, normw_ref, d)
    low_rank = _dot(xl, win_ref[:, 0:o_z])
    ckv = low_rank[:, o_kv:o_kr]
    kvn = _rms(ckv, kvw_ref[...], NORM_EPS).astype(BF16)
    knb = _dot(kvn, wk_ref[...]).astype(BF16)
    _store_rows(kn_ref, knb)
    _store_cols(vt_ref, _dot_nt(wvt_ref[...], kvn).astype(BF16))
    kr = low_rank[:, o_kr:o_z]
    if not is_ctx:
        cos, slo, shi = cos_ref[...], slo_ref[...], shi_ref[...]
        kr = _rope128(kr, cos, slo, shi)
    _store_rows(kr_ref, jnp.concatenate([kr, pltpu.roll(kr, LANES // 2, 1)], axis=1).astype(BF16))
    rope_ss = _sumsq(kr.astype(BF16))
    _store_rows(kss_ref, _lane_pack([_sumsq(knb[:, i * C_NOPE:(i + 1) * C_NOPE]) + rope_ss
                                     for i in range(C_HEADS)]))
    if is_ctx:
        return
    cq = low_rank[:, 0:o_kv]
    qn = _rms(cq, qw_ref[...], NORM_EPS).astype(BF16)
    q_ref[0, :, 0:nk] = (_dot(qn, wq_ref[:, 0:nk]) * scale).astype(BF16)
    qr = _dot(qn, wq_ref[:, nk:nk + nr]) * scale
    q_ref[0, :, nk:nk + nr] = jnp.concatenate(
        [_rope128(qr[:, i * LANES:(i + 1) * LANES], cos, slo, shi) for i in range(nr // LANES)],
        axis=1).astype(BF16)
    z_ref[0] = _silu(_dot(xl, win_ref[:, o_z:o_z + d])).astype(BF16)


def _odd_in(h, mod, norm_w, w_in, kv_norm_w, wk, wv_t, q_norm_w, wq, rope, *, tm, ctx_row, scale, total_rows,
            row_offset, alias, samples_per_tile=1):
    groups, rows, d = h.shape
    spt = samples_per_tile
    bsz = groups * spt
    sample_rows = tm // spt
    assert spt == 1 or rows == tm
    is_ctx = rope is None
    const = lambda *_: (0, 0)
    off = row_offset // sample_rows
    nk = C_HEADS * C_NOPE
    nq = nk + C_HEADS * C_ROPE
    in_specs = [
        pl.BlockSpec((1, tm, d), lambda b, j: (b, j, 0)),
        pl.BlockSpec(mod.shape, const),
        pl.BlockSpec((1, d), const),
        pl.BlockSpec(w_in.shape, const, pipeline_mode=pl.Buffered(1)),
        pl.BlockSpec(kv_norm_w.shape, const),
        pl.BlockSpec(wk.shape, const, pipeline_mode=pl.Buffered(1)),
        pl.BlockSpec(wv_t.shape, const, pipeline_mode=pl.Buffered(1)),
    ]
    args = [h, mod, norm_w, w_in, kv_norm_w, wk, wv_t]
    row_spec = lambda w: pl.BlockSpec((spt, sample_rows, w), lambda b, j: (b, j + off, 0))
    row_sds = lambda w, r: jax.ShapeDtypeStruct((bsz, r, w), BF16)
    out_specs = [row_spec(nk), row_spec(2 * LANES),
                 pl.BlockSpec((spt, nk, sample_rows), lambda b, j: (b, 0, j + off)), row_spec(LANES)]
    out_shape = [row_sds(nk, total_rows), row_sds(2 * LANES, total_rows),
                 jax.ShapeDtypeStruct((bsz, nk, total_rows), BF16),
                 jax.ShapeDtypeStruct((bsz, total_rows, LANES), F32)]
    aliases = {}
    if is_ctx:
        in_specs += [pl.BlockSpec(memory_space=pl.ANY)] * len(alias)
        aliases = {len(args) + i: i for i in range(len(alias))}
        args += list(alias)
    else:
        in_specs += [pl.BlockSpec(q_norm_w.shape, const),
                     pl.BlockSpec(wq.shape, const, pipeline_mode=pl.Buffered(1))]
        in_specs += [pl.BlockSpec((tm, LANES), lambda b, j: (j, 0))] * 3
        args += [q_norm_w, wq] + list(rope)
        tile = lambda w: pl.BlockSpec((1, tm, w), lambda b, j: (b, j, 0))
        out_specs += [tile(nq), tile(d)]
        out_shape += [row_sds(nq, rows), row_sds(d, rows)]
    blocks = (_nbytes((tm, d), F32) + _nbytes(w_in.shape, BF16) + 2 * _nbytes(wk.shape, BF16)
              + _nbytes((C_Q_RANK, nq), BF16) + _nbytes((tm, 2 * nk + 2 * LANES + nq + d), BF16)
              + _nbytes(mod.shape, F32) + 3 * _nbytes((tm, LANES), F32))
    return pl.pallas_call(
        functools.partial(_odd_in_kernel, is_ctx=is_ctx, ctx_row=ctx_row, d=d, scale=scale),
        grid=(groups, rows // tm),
        in_specs=in_specs,
        out_specs=out_specs,
        out_shape=out_shape,
        input_output_aliases=aliases,
        compiler_params=pltpu.CompilerParams(
            dimension_semantics=("arbitrary", "arbitrary"),
            vmem_limit_bytes=_vmem_limit(blocks, 12 * _nbytes((tm, 2 * nk), F32))),
        name="odd_in_ctx" if is_ctx else "odd_in_lat",
    )(*args)


def _odd_attn_kernel(h_ref, mod_ref, q_ref, z_ref, kn_ref, kr_ref, vt_ref, kss_ref, wout_ref, fw_ref,
                     o_ref, mix_ref, *, d, tiles_per_sample):
    t = pl.program_id(0)
    slot = lax.rem(t, 2)
    nk = C_HEADS * C_NOPE

    @pl.when(t == 0)
    def _():
        mix_ref[...] = jnp.zeros(mix_ref.shape, mix_ref.dtype)

    prev_sample = lax.div(jnp.maximum(t - 1, 0), tiles_per_sample)
    out = _dot(mix_ref[1 - slot], wout_ref[...])
    gate = mod_ref[pl.ds(prev_sample, 1), 2 * d:3 * d]
    h2 = h_ref[0] + gate * out
    o_ref[0] = _rms(h2, fw_ref[...], NORM_EPS)

    def q_head(hd):
        pair = nk + (hd // 2) * LANES
        return q_ref[0, :, hd * C_NOPE:(hd + 1) * C_NOPE], q_ref[0, :, pair:pair + LANES]

    def run(exact):
        if not exact:
            low = lax.broadcasted_iota(jnp.int32, (q_ref.shape[1], LANES), 1) < C_ROPE
            qss = _lane_pack([_sumsq(q_head(hd)[0]) + _sumsq(q_head(hd)[1], low if hd % 2 == 0 else ~low)
                              for hd in range(C_HEADS)])
            bounds = _score_bounds(qss, kss_ref)
        lmin = None
        for hd in range(C_HEADS):
            cs = slice(hd * C_NOPE, (hd + 1) * C_NOPE)
            qh = jnp.concatenate(q_head(hd), axis=1)
            rs = slice((hd % 2) * LANES, (hd % 2 + 1) * LANES)
            kh = jnp.concatenate([kn_ref[0, :, cs], kr_ref[0, :, rs]], axis=1)
            if exact:
                o_t = _attend_t(kh, qh, vt_ref[0, cs, :])
            else:
                o_t, l = _attend_t_bounded(kh, qh, vt_ref[0, cs, :], bounds[hd:hd + 1])
                lmin = l if lmin is None else jnp.minimum(lmin, l)
            mix_ref[slot, :, cs] = (o_t.T * z_ref[0, :, cs].astype(F32)).astype(BF16)
        return lmin

    lmin = run(exact=False)

    @pl.when(jnp.logical_not(jnp.min(lmin) >= L_FLOOR))
    def _():
        run(exact=True)


def _odd_attn(h, mod, q, z, kn, kr, v_t, kss, w_out, final_w, *, tq):
    bsz, rows, d = h.shape
    nq = rows // tq
    n_tiles = bsz * nq
    const = lambda *_: (0, 0)
    cur = lambda t: jnp.minimum(t, n_tiles - 1)
    prev = lambda t: jnp.maximum(t - 1, 0)
    cur_tile = lambda w: pl.BlockSpec((1, tq, w), lambda t: (cur(t) // nq, cur(t) % nq, 0))
    prev_tile = lambda w: pl.BlockSpec((1, tq, w), lambda t: (prev(t) // nq, prev(t) % nq, 0))
    full = lambda arr: pl.BlockSpec((1,) + arr.shape[1:], lambda t: (cur(t) // nq, 0, 0))
    kv_arrays = [kn, kr, v_t, kss]
    n_keys = kn.shape[1]
    in_specs = ([prev_tile(d), pl.BlockSpec(mod.shape, const), cur_tile(q.shape[-1]), cur_tile(d)]
                + [full(a) for a in kv_arrays]
                + [pl.BlockSpec(w_out.shape, const, pipeline_mode=pl.Buffered(1)), pl.BlockSpec((1, d), const)])
    blocks = (2 * _nbytes((tq, d), F32) + _nbytes((tq, q.shape[-1] + d), BF16)
              + sum(_nbytes(a.shape[1:], a.dtype) for a in kv_arrays)
              + _nbytes(w_out.shape, BF16) + _nbytes(mod.shape, F32))
    temps = 4 * _nbytes((n_keys, tq), F32) + 2 * _nbytes((tq, d), BF16)
    return pl.pallas_call(
        functools.partial(_odd_attn_kernel, d=d, tiles_per_sample=nq),
        grid=(n_tiles + 1,),
        in_specs=in_specs,
        out_specs=prev_tile(d),
        out_shape=jax.ShapeDtypeStruct((bsz, rows, d), F32),
        scratch_shapes=[pltpu.VMEM((2, tq, d), BF16)],
        compiler_params=pltpu.CompilerParams(
            dimension_semantics=("arbitrary",),
            vmem_limit_bytes=_vmem_limit(blocks, temps + TEMP_HEADROOM)),
        name="odd_attn",
    )(h, mod, q, z, *kv_arrays, w_out, final_w)


def kernel(x, c, ctx, c_ctx, norm_w, ada_w, ada_b, even_w_in, a_ws, a_bs, a_ln_w, a_ln_b, b_lq1, b_lk1,
           b_lq2, b_lk2, b_subln_w, even_w_out, odd_w_in, c_q_norm_w, c_wq_b, c_kv_norm_w, c_wkv_b,
           odd_w_out, final_w):
    bsz, seq, d = x.shape
    clen = ctx.shape[1]
    ntok = seq + clen
    assert bsz + 1 <= MOD_ROWS and seq % GRID_W == 0 and B_HEAD_DIM == C_ROPE and seq % clen == 0
    assert bsz % CTX_GROUP == 0
    ctx_row = bsz

    cond = jnp.concatenate([c, c_ctx[None, :], jnp.zeros((MOD_ROWS - bsz - 1, d), F32)], axis=0)
    mod = _modulation(cond, ada_w, ada_b)
    rope = _rope_tables(seq, C_ROPE)

    aw = a_ln_w.shape[-1]
    w_in0 = even_w_in[0].astype(BF16)
    w_vt0 = _transposed_columns(even_w_in, 5 * aw, aw)
    ws = a_ws[0].astype(BF16)
    ws_pairs = jnp.concatenate([ws[0::2], ws[1::2]], axis=2)
    bias = jnp.repeat(a_bs[0].T, aw // A_GROUPS, axis=1)
    ln_w, ln_b = a_ln_w[0][None, :], a_ln_b[0][None, :]
    lparams = [p[0][None, :] for p in (b_lq1, b_lk1, b_lq2, b_lk2)]
    subln_w = b_subln_w[0][None, :]
    w_out0 = even_w_out[0].astype(BF16)
    lam_init = 0.8 - 0.6 * math.exp(-0.3 * 0)
    nw0 = norm_w[0][None, :]

    even_lat = _even_in(x, mod[0], nw0, w_in0, ws_pairs, bias, ln_w, ln_b, w_vt0, rope,
                        tm=PROJ_ROWS, ctx_row=None, total_rows=ntok, row_offset=0, alias=None)
    grouped = lambda t: t.reshape(bsz // CTX_GROUP, CTX_GROUP * clen, d)
    a_t, q_t, k_t, vt_t, z_t, kss_t = _even_in(grouped(ctx), mod[0], nw0, w_in0, ws_pairs, bias, ln_w, ln_b, w_vt0,
                                               None, tm=CTX_GROUP * clen, ctx_row=ctx_row, total_rows=ntok,
                                               row_offset=seq, alias=even_lat, samples_per_tile=CTX_GROUP)
    h_lat = _even_attn(x, mod[0], a_t, q_t, z_t, k_t, vt_t, kss_t, lparams, subln_w, w_out0,
                       tq=ATTN_ROWS, ctx_row=None, lam_init=lam_init, q_offset=0, key_start=0, n_keys=ntok)
    h_ctx = _even_attn(ctx, mod[0], a_t, q_t, z_t, k_t, vt_t, kss_t, lparams, subln_w, w_out0,
                       tq=clen, ctx_row=ctx_row, lam_init=lam_init, q_offset=seq, key_start=seq, n_keys=clen)

    w1 = odd_w_in[0]
    o_kr = C_Q_RANK + C_KV_RANK
    w_in1 = jnp.concatenate([w1[:, :o_kr + C_ROPE], jnp.zeros((d, LANES - C_ROPE), F32),
                             w1[:, o_kr + C_ROPE:]], axis=1).astype(BF16)
    wq = c_wq_b[0].reshape(C_Q_RANK, C_HEADS, C_NOPE + C_ROPE)
    wq = jnp.concatenate([wq[:, :, :C_NOPE].reshape(C_Q_RANK, -1),
                          wq[:, :, C_NOPE:].reshape(C_Q_RANK, -1)], axis=1).astype(BF16)
    wkv = c_wkv_b[0].reshape(C_KV_RANK, C_HEADS, C_NOPE + C_V)
    wk = wkv[:, :, :C_NOPE].reshape(C_KV_RANK, -1).astype(BF16)
    wv_t = wkv[:, :, C_NOPE:].reshape(C_KV_RANK, -1).T.astype(BF16)
    qnw, kvnw = c_q_norm_w[0][None, :], c_kv_norm_w[0][None, :]
    w_out1 = odd_w_out[0].astype(BF16)
    nw1 = norm_w[1][None, :]
    scale = (C_NOPE + C_ROPE) ** -0.5 * LOG2E

    kn_l, kr_l, vt_l, kss_l, qq, zz = _odd_in(h_lat, mod[1], nw1, w_in1, kvnw, wk, wv_t, qnw, wq, rope,
                                              tm=PROJ_ROWS, ctx_row=None, scale=scale, total_rows=ntok, row_offset=0,
                                              alias=None)
    kn, kr, vt, kss = _odd_in(grouped(h_ctx), mod[1], nw1, w_in1, kvnw, wk, wv_t, None, None, None,
                              tm=CTX_GROUP * clen, ctx_row=ctx_row, scale=scale, total_rows=ntok, row_offset=seq,
                              alias=(kn_l, kr_l, vt_l, kss_l), samples_per_tile=CTX_GROUP)
    return _odd_attn(h_lat, mod[1], qq, zz, kn, kr, vt, kss, w_out1, final_w[None, :], tq=ATTN_ROWS)
```
